```python
import math
import jax, jax.numpy as jnp
from jax import lax
import numpy as np

D_MODEL = 4096
BATCH = 4
SEQ = 2048
DEPTH = 2

HEAD_DIM = 128
NORM_EPS = 1e-6
ROPE_THETA = 10000.0
SWA_Q_HEADS = 16
SWA_KV_HEADS = 4
SWA_GROUP = SWA_Q_HEADS // SWA_KV_HEADS
SWA_WINDOW = 128
SWA_BLOCK = SWA_WINDOW
GDN_HEADS = 8
GDN_CONV = 4
GDN_CHUNK = 64
RET_HEADS = 8
RET_CHUNK = 128
D_FF = 11008
FFN_CONV = 3
N_BRANCHES = 3

SWA_WIDTH = SWA_Q_HEADS * HEAD_DIM
SWA_KV_WIDTH = SWA_KV_HEADS * HEAD_DIM
GDN_WIDTH = GDN_HEADS * HEAD_DIM
RET_WIDTH = RET_HEADS * HEAD_DIM
MIX_WIDTH = SWA_WIDTH + GDN_WIDTH + RET_WIDTH
IN_SPLIT = (SWA_WIDTH, SWA_KV_WIDTH, SWA_KV_WIDTH,
            GDN_WIDTH, GDN_WIDTH, GDN_WIDTH, GDN_WIDTH, GDN_HEADS, GDN_HEADS,
            RET_WIDTH, RET_WIDTH, RET_WIDTH, RET_WIDTH,
            N_BRANCHES * D_MODEL)
IN_WIDTH = sum(IN_SPLIT)

kernel_name = "hybrid_swa_gdn_retention_convglu"


def rms_norm(x, w):
    xf = x.astype(jnp.float32)
    y = xf * lax.rsqrt(jnp.mean(xf * xf, axis=-1, keepdims=True) + NORM_EPS)
    return (y * w.astype(jnp.float32)).astype(x.dtype)


def l2_normalize(t):
    return t * lax.rsqrt(jnp.sum(t * t, axis=-1, keepdims=True) + NORM_EPS)


def rotary(t, pos):
    half = t.shape[-1] // 2
    inv_freq = ROPE_THETA ** (-jnp.arange(half, dtype=jnp.float32) / half)
    ang = pos.astype(jnp.float32)[:, None] * inv_freq[None, :]
    cos = jnp.cos(ang)[None, :, None, :]
    sin = jnp.sin(ang)[None, :, None, :]
    t1, t2 = t[..., :half], t[..., half:]
    return jnp.concatenate([t1 * cos - t2 * sin, t2 * cos + t1 * sin], axis=-1)


def causal_depthwise_conv(x, w):
    k_width, ch = w.shape
    return lax.conv_general_dilated(
        x, w[:, None, :].astype(x.dtype), window_strides=(1,), padding=[(k_width - 1, 0)],
        dimension_numbers=('NWC', 'WIO', 'NWC'), feature_group_count=ch)


def to_chunks(t, chunk):
    b, s, h = t.shape[:3]
    t = t.reshape((b, s // chunk, chunk, h) + t.shape[3:])
    return jnp.moveaxis(t, 3, 1)


def from_chunks(t):
    b, h, n, c = t.shape[:4]
    return jnp.moveaxis(t, 1, 3).reshape((b, n * c, h) + t.shape[4:])


def sliding_window_attention(q, k, v, sinks):
    b, s, _, d = q.shape
    nb = s // SWA_BLOCK
    qb = q.reshape(b, nb, SWA_BLOCK, SWA_KV_HEADS, SWA_GROUP, d)

    def banded(t):
        prev = jnp.pad(t, ((0, 0), (SWA_BLOCK, 0), (0, 0), (0, 0)))[:, :s]
        return jnp.concatenate([prev.reshape(b, nb, SWA_BLOCK, SWA_KV_HEADS, d),
                                t.reshape(b, nb, SWA_BLOCK, SWA_KV_HEADS, d)], axis=2)

    kb, vb = banded(k), banded(v)
    scores = jnp.einsum('bnqhgd,bnkhd->bnhgqk', qb, kb) * (d ** -0.5)
    qi = jnp.arange(SWA_BLOCK)[:, None]
    kj = jnp.arange(2 * SWA_BLOCK)[None, :]
    dist = qi + SWA_BLOCK - kj
    in_window = (dist >= 0) & (dist < SWA_WINDOW)
    first = (jnp.arange(nb) == 0)[:, None, None]
    valid = in_window[None] & ~(first & (kj < SWA_BLOCK)[None])
    scores = jnp.where(valid[None, :, None, None], scores, -jnp.inf)
    sink = sinks.astype(jnp.float32).reshape(1, 1, SWA_KV_HEADS, SWA_GROUP, 1, 1)
    sink = jnp.broadcast_to(sink, scores.shape[:-1] + (1,))
    probs = jax.nn.softmax(jnp.concatenate([scores, sink], axis=-1), axis=-1)[..., :-1]
    out = jnp.einsum('bnhgqk,bnkhd->bnqhgd', probs, vb)
    return out.reshape(b, s, SWA_Q_HEADS * d)


def gated_delta_rule(q, k, v, g, beta):
    b, s, h, dk = q.shape
    dv = v.shape[-1]
    c = GDN_CHUNK
    q = l2_normalize(q) * (dk ** -0.5)
    k = l2_normalize(k)
    q, k, v, g, beta = (to_chunks(t, c) for t in (q, k, v, g, beta))
    g_cum = jnp.cumsum(g, axis=-1)
    causal = jnp.tril(jnp.ones((c, c), dtype=bool))
    strict = jnp.tril(jnp.ones((c, c), dtype=bool), -1)
    decay = jnp.exp(jnp.where(causal, g_cum[..., :, None] - g_cum[..., None, :], -jnp.inf))
    kk = jnp.einsum('bhnid,bhnjd->bhnij', k, k)
    lower = jnp.where(strict, beta[..., :, None] * kk * decay, 0.0)
    a_mat = lower + jnp.eye(c, dtype=jnp.float32)
    rhs = jnp.concatenate([v * beta[..., None], k * (beta * jnp.exp(g_cum))[..., None]], axis=-1)
    sol = lax.linalg.triangular_solve(a_mat, rhs, left_side=True, lower=True, unit_diagonal=True)
    u, w = sol[..., :dv], sol[..., dv:]
    qk = jnp.einsum('bhnid,bhnjd->bhnij', q, k) * decay

    def step(state, xs):
        qc, kc, uc, wc, gc, ac = xs
        v_new = uc - jnp.einsum('bhck,bhkv->bhcv', wc, state)
        out = (jnp.einsum('bhck,bhkv->bhcv', qc * jnp.exp(gc)[..., None], state)
               + jnp.einsum('bhcs,bhsv->bhcv', ac, v_new))
        g_last = gc[..., -1:]
        state = (state * jnp.exp(g_last)[..., None]
                 + jnp.einsum('bhck,bhcv->bhkv', kc * jnp.exp(g_last - gc)[..., None], v_new))
        return state, out

    xs = tuple(jnp.moveaxis(t, 2, 0) for t in (q, k, u, w, g_cum, qk))
    state0 = jnp.zeros((b, h, dk, dv), jnp.float32)
    _, out = lax.scan(step, state0, xs)
    return from_chunks(jnp.moveaxis(out, 0, 2))


def multiscale_retention(q, k, v):
    b, s, h, dk = q.shape
    c = RET_CHUNK
    log_gamma = jnp.log(1.0 - 2.0 ** (-5.0 - jnp.arange(h, dtype=jnp.float32)))
    idx = jnp.arange(c, dtype=jnp.float32)
    rel = idx[:, None] - idx[None, :]
    d_mask = jnp.where(rel >= 0, jnp.exp(jnp.maximum(rel, 0.0)[None] * log_gamma[:, None, None]), 0.0)
    xi = jnp.exp((idx + 1.0)[None, :] * log_gamma[:, None])
    zeta = jnp.exp((c - 1.0 - idx)[None, :] * log_gamma[:, None])
    gamma_chunk = jnp.exp(c * log_gamma)
    k = k * (dk ** -0.5)
    q, k, v = (to_chunks(t, c) for t in (q, k, v))
    scores = jnp.einsum('bhnid,bhnjd->bhnij', q, k) * d_mask[None, :, None]
    inner = jnp.einsum('bhnij,bhnjv->bhniv', scores, v)

    def step(state, xs):
        qc, kc, vc = xs
        cross = jnp.einsum('bhck,bhkv->bhcv', qc, state) * xi[None, :, :, None]
        state = (state * gamma_chunk[None, :, None, None]
                 + jnp.einsum('bhck,bhcv->bhkv', kc * zeta[None, :, :, None], vc))
        return state, cross

    xs = tuple(jnp.moveaxis(t, 2, 0) for t in (q, k, v))
    state0 = jnp.zeros((b, h, dk, v.shape[-1]), jnp.float32)
    _, cross = lax.scan(step, state0, xs)
    return from_chunks(inner + jnp.moveaxis(cross, 0, 2))


def hybrid_mixer(h, w_in, gdn_conv_w, gdn_a_log, gdn_dt_bias, gdn_norm_w, attn_sinks, w_branch, w_out):
    b, s, _ = h.shape
    f32 = jnp.float32
    dt = h.dtype
    split_points = np.cumsum(IN_SPLIT)[:-1].tolist()
    (a_q, a_k, a_v, b_q, b_k, b_v, b_z, b_beta, b_alpha,
     c_q, c_k, c_v, c_g, gate_logits) = jnp.split(h @ w_in, split_points, axis=-1)
    pos = jnp.arange(s)

    qa = rotary(a_q.reshape(b, s, SWA_Q_HEADS, HEAD_DIM).astype(f32), pos)
    ka = rotary(a_k.reshape(b, s, SWA_KV_HEADS, HEAD_DIM).astype(f32), pos)
    va = a_v.reshape(b, s, SWA_KV_HEADS, HEAD_DIM).astype(f32)
    out_a = sliding_window_attention(qa, ka, va, attn_sinks)

    qkv = jax.nn.silu(causal_depthwise_conv(jnp.concatenate([b_q, b_k, b_v], axis=-1), gdn_conv_w)).astype(f32)
    qb, kb, vb = jnp.split(qkv, 3, axis=-1)
    beta = jax.nn.sigmoid(b_beta.astype(f32))
    g = -jnp.exp(gdn_a_log.astype(f32)) * jax.nn.softplus(b_alpha.astype(f32) + gdn_dt_bias.astype(f32))
    ob = gated_delta_rule(qb.reshape(b, s, GDN_HEADS, HEAD_DIM), kb.reshape(b, s, GDN_HEADS, HEAD_DIM),
                          vb.reshape(b, s, GDN_HEADS, HEAD_DIM), g, beta)
    ob = rms_norm(ob, gdn_norm_w) * jax.nn.silu(b_z.reshape(b, s, GDN_HEADS, HEAD_DIM).astype(f32))
    out_b = ob.reshape(b, s, GDN_WIDTH)

    qc = rotary(c_q.reshape(b, s, RET_HEADS, HEAD_DIM).astype(f32), pos)
    kc = rotary(c_k.reshape(b, s, RET_HEADS, HEAD_DIM).astype(f32), pos)
    oc = multiscale_retention(qc, kc, c_v.reshape(b, s, RET_HEADS, HEAD_DIM).astype(f32))
    mu = jnp.mean(oc, axis=-1, keepdims=True)
    var = jnp.mean(jnp.square(oc - mu), axis=-1, keepdims=True)
    oc = (oc - mu) * lax.rsqrt(var + NORM_EPS)
    out_c = jax.nn.silu(c_g.astype(f32)) * oc.reshape(b, s, RET_WIDTH)

    gates = jax.nn.sigmoid(gate_logits.astype(f32)).reshape(b, s, N_BRANCHES, D_MODEL)
    p_a = out_a.astype(dt) @ w_branch[:SWA_WIDTH]
    p_b = out_b.astype(dt) @ w_branch[SWA_WIDTH:SWA_WIDTH + GDN_WIDTH]
    p_c = out_c.astype(dt) @ w_branch[SWA_WIDTH + GDN_WIDTH:]
    merged = gates[:, :, 0] * p_a + gates[:, :, 1] * p_b + gates[:, :, 2] * p_c
    return merged.astype(dt) @ w_out


def conv_glu_ffn(h, w_up, conv_w, conv_b, w_down):
    u = causal_depthwise_conv(h @ w_up, conv_w) + conv_b
    gate, up = jnp.split(u, 2, axis=-1)
    return (jax.nn.silu(gate) * up) @ w_down


def setup_inputs(seed: int = 0) -> dict:
    key = jax.random.key(seed)
    ks = jax.random.split(key, 20)
    f32 = jnp.float32

    def dense(k, fan_in, shape):
        return jax.random.normal(k, shape, f32) * (fan_in ** -0.5)

    def gain(k, shape):
        return 1.0 + 0.01 * jax.random.normal(k, shape, f32)

    x = jax.random.normal(ks[0], (BATCH, SEQ, D_MODEL), f32)
    ln_mix_w = gain(ks[1], (DEPTH, D_MODEL))
    ln_ffn_w = gain(ks[2], (DEPTH, D_MODEL))
    w_in = dense(ks[3], D_MODEL, (DEPTH, D_MODEL, IN_WIDTH))
    gdn_conv_w = dense(ks[4], GDN_CONV, (DEPTH, GDN_CONV, 3 * GDN_WIDTH))
    gdn_a_log = jnp.log(jax.random.uniform(ks[5], (DEPTH, GDN_HEADS), f32, 1.0, 16.0))
    dt0 = jnp.exp(jax.random.uniform(ks[6], (DEPTH, GDN_HEADS), f32, math.log(1e-3), math.log(1e-1)))
    gdn_dt_bias = dt0 + jnp.log(-jnp.expm1(-dt0))
    gdn_norm_w = gain(ks[7], (DEPTH, HEAD_DIM))
    attn_sinks = jax.random.normal(ks[8], (DEPTH, SWA_Q_HEADS), f32)
    w_branch = jnp.concatenate([
        dense(ks[9], SWA_WIDTH, (DEPTH, SWA_WIDTH, D_MODEL)),
        dense(ks[10], GDN_WIDTH, (DEPTH, GDN_WIDTH, D_MODEL)),
        dense(ks[11], RET_WIDTH, (DEPTH, RET_WIDTH, D_MODEL))], axis=1)
    w_out = dense(ks[12], D_MODEL, (DEPTH, D_MODEL, D_MODEL))
    w_up = dense(ks[13], D_MODEL, (DEPTH, D_MODEL, 2 * D_FF))
    ffn_conv_w = dense(ks[14], FFN_CONV, (DEPTH, FFN_CONV, 2 * D_FF))
    ffn_conv_b = 0.01 * jax.random.normal(ks[15], (DEPTH, 2 * D_FF), f32)
    w_down = dense(ks[16], D_FF, (DEPTH, D_FF, D_MODEL))
    ln_final_w = gain(ks[17], (D_MODEL,))
    return {"x": x, "ln_mix_w": ln_mix_w, "ln_ffn_w": ln_ffn_w, "w_in": w_in,
            "gdn_conv_w": gdn_conv_w, "gdn_a_log": gdn_a_log, "gdn_dt_bias": gdn_dt_bias,
            "gdn_norm_w": gdn_norm_w, "attn_sinks": attn_sinks, "w_branch": w_branch,
            "w_out": w_out, "w_up": w_up, "ffn_conv_w": ffn_conv_w, "ffn_conv_b": ffn_conv_b,
            "w_down": w_down, "ln_final_w": ln_final_w}


def reference(x, ln_mix_w, ln_ffn_w, w_in, gdn_conv_w, gdn_a_log, gdn_dt_bias, gdn_norm_w,
              attn_sinks, w_branch, w_out, w_up, ffn_conv_w, ffn_conv_b, w_down, ln_final_w):
    for layer in range(DEPTH):
        h = rms_norm(x, ln_mix_w[layer])
        x = x + hybrid_mixer(h, w_in[layer], gdn_conv_w[layer], gdn_a_log[layer], gdn_dt_bias[layer],
                             gdn_norm_w[layer], attn_sinks[layer], w_branch[layer], w_out[layer])
        h = rms_norm(x, ln_ffn_w[layer])
        x = x + conv_glu_ffn(h, w_up[layer], ffn_conv_w[layer], ffn_conv_b[layer], w_down[layer])
    return rms_norm(x, ln_final_w)
```

```python
import functools
import math

import jax
import jax.numpy as jnp
import numpy as np
from jax import lax
from jax.experimental import pallas as pl
from jax.experimental.pallas import tpu as pltpu

HEAD_DIM = 128
NORM_EPS = 1e-6
ROPE_THETA = 10000.0
SWA_Q_HEADS = 16
SWA_KV_HEADS = 4
SWA_GROUP = SWA_Q_HEADS // SWA_KV_HEADS
SWA_WINDOW = 128
SWA_BLOCK = SWA_WINDOW
GDN_HEADS = 8
GDN_CONV = 4
GDN_CHUNK = 64
RET_HEADS = 8
RET_CHUNK = 128
FFN_CONV = 3
N_BRANCHES = 3

SWA_WIDTH = SWA_Q_HEADS * HEAD_DIM
SWA_KV_WIDTH = SWA_KV_HEADS * HEAD_DIM
GDN_WIDTH = GDN_HEADS * HEAD_DIM
RET_WIDTH = RET_HEADS * HEAD_DIM
MIX_WIDTH = SWA_WIDTH + GDN_WIDTH + RET_WIDTH

A_OFF = 0
B_OFF = SWA_WIDTH + 2 * SWA_KV_WIDTH
C_OFF = B_OFF + 4 * GDN_WIDTH
G_OFF = C_OFF + 4 * RET_WIDTH
BA_WIDTH = 128

VMEM_LIMIT_BYTES = 52 * 1024 * 1024
FFN_TILE = 256


def _cparams(*sem):
    return pltpu.CompilerParams(dimension_semantics=sem, vmem_limit_bytes=VMEM_LIMIT_BYTES)


def _rmsnorm_kernel(x_ref, w_ref, o_ref):
    x = x_ref[...]
    y = x * lax.rsqrt(jnp.mean(x * x, axis=-1, keepdims=True) + NORM_EPS)
    o_ref[...] = (y * w_ref[...]).astype(o_ref.dtype)


def _rmsnorm(x, w, out_dtype, tm=256):
    t, d = x.shape
    return pl.pallas_call(
        _rmsnorm_kernel,
        grid=(t // tm,),
        in_specs=[pl.BlockSpec((tm, d), lambda i: (i, 0)),
                  pl.BlockSpec((1, d), lambda i: (0, 0))],
        out_specs=pl.BlockSpec((tm, d), lambda i: (i, 0)),
        out_shape=jax.ShapeDtypeStruct((t, d), out_dtype),
        compiler_params=_cparams("parallel"),
        name="rmsnorm",
    )(x, w.reshape(1, d))


def _mm_kernel(x_ref, w_ref, o_ref):
    o_ref[...] = jnp.dot(x_ref[...], w_ref[...],
                         preferred_element_type=jnp.float32).astype(o_ref.dtype)


def _matmul(x, w, out_dtype, tm, tn, name):
    m, k = x.shape
    n = w.shape[1]
    return pl.pallas_call(
        _mm_kernel,
        grid=(m // tm, n // tn),
        in_specs=[pl.BlockSpec((tm, k), lambda i, j: (i, 0)),
                  pl.BlockSpec((k, tn), lambda i, j: (0, j))],
        out_specs=pl.BlockSpec((tm, tn), lambda i, j: (i, j)),
        out_shape=jax.ShapeDtypeStruct((m, n), out_dtype),
        compiler_params=_cparams("parallel", "arbitrary"),
        name=name,
    )(x, w)


def _mm_res_kernel(x_ref, w_ref, r_ref, o_ref, acc_ref, *, nk):
    kk = pl.program_id(2)
    part = jnp.dot(x_ref[...], w_ref[...], preferred_element_type=jnp.float32)
    if nk == 1:
        o_ref[...] = r_ref[...] + part
    else:
        @pl.when(kk == 0)
        def _():
            acc_ref[...] = part

        @pl.when(jnp.logical_and(kk > 0, kk < nk - 1))
        def _():
            acc_ref[...] += part

        @pl.when(kk == nk - 1)
        def _():
            o_ref[...] = r_ref[...] + (acc_ref[...] + part)


def _matmul_residual(x, w, res, tm, tn, nk, name):
    m, k = x.shape
    n = w.shape[1]
    tk = k // nk
    return pl.pallas_call(
        functools.partial(_mm_res_kernel, nk=nk),
        grid=(m // tm, n // tn, nk),
        in_specs=[pl.BlockSpec((tm, tk), lambda i, j, kk: (i, kk)),
                  pl.BlockSpec((tk, tn), lambda i, j, kk: (kk, j)),
                  pl.BlockSpec((tm, tn), lambda i, j, kk: (i, j))],
        out_specs=pl.BlockSpec((tm, tn), lambda i, j, kk: (i, j)),
        out_shape=jax.ShapeDtypeStruct((m, n), jnp.float32),
        scratch_shapes=[pltpu.VMEM((tm, tn), jnp.float32)],
        compiler_params=_cparams("parallel", "arbitrary", "arbitrary"),
        name=name,
    )(x, w, res)


def _merge_kernel(a_ref, b_ref, c_ref, wa_ref, wb_ref, wc_ref, ga_ref, gb_ref, gc_ref, o_ref):
    def branch(x_ref, w_ref, g_ref):
        p = jnp.dot(x_ref[...], w_ref[...], preferred_element_type=jnp.float32)
        return jax.nn.sigmoid(g_ref[...].astype(jnp.float32)) * p

    merged = branch(a_ref, wa_ref, ga_ref) + branch(b_ref, wb_ref, gb_ref) + branch(c_ref, wc_ref, gc_ref)
    o_ref[...] = merged.astype(o_ref.dtype)


def _branch_merge(out_a, out_b, out_c, w_branch, proj, d_model, tm, tn):
    t = out_a.shape[0]
    gate_blk = G_OFF // tn
    d_blk = d_model // tn
    b_row = SWA_WIDTH // GDN_WIDTH
    return pl.pallas_call(
        _merge_kernel,
        grid=(t // tm, d_model // tn),
        in_specs=[pl.BlockSpec((tm, SWA_WIDTH), lambda i, j: (i, 0)),
                  pl.BlockSpec((tm, GDN_WIDTH), lambda i, j: (i, 0)),
                  pl.BlockSpec((tm, RET_WIDTH), lambda i, j: (i, 0)),
                  pl.BlockSpec((SWA_WIDTH, tn), lambda i, j: (0, j)),
                  pl.BlockSpec((GDN_WIDTH, tn), lambda i, j: (b_row, j)),
                  pl.BlockSpec((RET_WIDTH, tn), lambda i, j: (b_row + 1, j)),
                  pl.BlockSpec((tm, tn), lambda i, j: (i, gate_blk + j)),
                  pl.BlockSpec((tm, tn), lambda i, j: (i, gate_blk + d_blk + j)),
                  pl.BlockSpec((tm, tn), lambda i, j: (i, gate_blk + 2 * d_blk + j))],
        out_specs=pl.BlockSpec((tm, tn), lambda i, j: (i, j)),
        out_shape=jax.ShapeDtypeStruct((t, d_model), jnp.bfloat16),
        compiler_params=_cparams("parallel", "arbitrary"),
        name="branch_merge",
    )(out_a, out_b, out_c, w_branch, w_branch, w_branch, proj, proj, proj)


def _upglu_kernel(x_ref, w_ref, cw_ref, cb_ref, o_ref, u_ref, *, tm, tiles_per_seq):
    mi = pl.program_id(1)

    @pl.when(mi % tiles_per_seq == 0)
    def _():
        u_ref[0:8, :] = jnp.zeros((8, u_ref.shape[1]), jnp.float32)

    u_ref[8:8 + tm, :] = jnp.dot(x_ref[...], w_ref[...], preferred_element_type=jnp.float32)
    cw = cw_ref[...]
    y = cb_ref[...] + cw[2:3, :] * u_ref[8:8 + tm, :]
    y = y + cw[1:2, :] * u_ref[7:7 + tm, :]
    y = y + cw[0:1, :] * u_ref[6:6 + tm, :]
    u_ref[0:8, :] = u_ref[tm:tm + 8, :]
    gate = y[:, :FFN_TILE]
    up = y[:, FFN_TILE:]
    o_ref[...] = (gate * jax.nn.sigmoid(gate) * up).astype(o_ref.dtype)


def _up_glu(h, w_up_p, conv_w_p, conv_b_p, seq, tm):
    t, d = h.shape
    n2 = w_up_p.shape[1]
    nblk = n2 // (2 * FFN_TILE)
    return pl.pallas_call(
        functools.partial(_upglu_kernel, tm=tm, tiles_per_seq=seq // tm),
        grid=(nblk, t // tm),
        in_specs=[pl.BlockSpec((tm, d), lambda j, i: (i, 0)),
                  pl.BlockSpec((d, 2 * FFN_TILE), lambda j, i: (0, j)),
                  pl.BlockSpec((FFN_CONV, 2 * FFN_TILE), lambda j, i: (0, j)),
                  pl.BlockSpec((1, 2 * FFN_TILE), lambda j, i: (0, j))],
        out_specs=pl.BlockSpec((tm, FFN_TILE), lambda j, i: (i, j)),
        out_shape=jax.ShapeDtypeStruct((t, nblk * FFN_TILE), jnp.bfloat16),
        scratch_shapes=[pltpu.VMEM((tm + 8, 2 * FFN_TILE), jnp.float32)],
        compiler_params=_cparams("parallel", "arbitrary"),
        name="up_conv_glu",
    )(h, w_up_p, conv_w_p, conv_b_p)


def _interleave_gate_up(a):
    lead = a.shape[:-1]
    f = a.shape[-1] // 2
    a = a.reshape(lead + (2, f // FFN_TILE, FFN_TILE))
    a = jnp.swapaxes(a, -3, -2)
    return a.reshape(lead + (2 * f,))


def _rotary(t, pos):
    half = t.shape[-1] // 2
    inv_freq = ROPE_THETA ** (-jnp.arange(half, dtype=jnp.float32) / half)
    ang = pos.astype(jnp.float32)[:, None] * inv_freq[None, :]
    cos = jnp.cos(ang)[None, :, None, :]
    sin = jnp.sin(ang)[None, :, None, :]
    t1, t2 = t[..., :half], t[..., half:]
    return jnp.concatenate([t1 * cos - t2 * sin, t2 * cos + t1 * sin], axis=-1)


def _l2_normalize(t):
    return t * lax.rsqrt(jnp.sum(t * t, axis=-1, keepdims=True) + NORM_EPS)


def _causal_depthwise_conv(x, w):
    k_width, ch = w.shape
    return lax.conv_general_dilated(
        x, w[:, None, :].astype(x.dtype), window_strides=(1,), padding=[(k_width - 1, 0)],
        dimension_numbers=('NWC', 'WIO', 'NWC'), feature_group_count=ch)


def _to_chunks(t, chunk):
    b, s, h = t.shape[:3]
    t = t.reshape((b, s // chunk, chunk, h) + t.shape[3:])
    return jnp.moveaxis(t, 3, 1)


def _from_chunks(t):
    b, h, n, c = t.shape[:4]
    return jnp.moveaxis(t, 1, 3).reshape((b, n * c, h) + t.shape[4:])


def _sliding_window_attention(q, k, v, sinks):
    b, s, _, d = q.shape
    nb = s // SWA_BLOCK
    qb = q.reshape(b, nb, SWA_BLOCK, SWA_KV_HEADS, SWA_GROUP, d)

    def banded(t):
        prev = jnp.pad(t, ((0, 0), (SWA_BLOCK, 0), (0, 0), (0, 0)))[:, :s]
        return jnp.concatenate([prev.reshape(b, nb, SWA_BLOCK, SWA_KV_HEADS, d),
                                t.reshape(b, nb, SWA_BLOCK, SWA_KV_HEADS, d)], axis=2)

    kb, vb = banded(k), banded(v)
    scores = jnp.einsum('bnqhgd,bnkhd->bnhgqk', qb, kb) * (d ** -0.5)
    qi = jnp.arange(SWA_BLOCK)[:, None]
    kj = jnp.arange(2 * SWA_BLOCK)[None, :]
    dist = qi + SWA_BLOCK - kj
    in_window = (dist >= 0) & (dist < SWA_WINDOW)
    first = (jnp.arange(nb) == 0)[:, None, None]
    valid = in_window[None] & ~(first & (kj < SWA_BLOCK)[None])
    scores = jnp.where(valid[None, :, None, None], scores, -jnp.inf)
    sink = sinks.astype(jnp.float32).reshape(1, 1, SWA_KV_HEADS, SWA_GROUP, 1, 1)
    sink = jnp.broadcast_to(sink, scores.shape[:-1] + (1,))
    probs = jax.nn.softmax(jnp.concatenate([scores, sink], axis=-1), axis=-1)[..., :-1]
    out = jnp.einsum('bnhgqk,bnkhd->bnqhgd', probs, vb)
    return out.reshape(b, s, SWA_Q_HEADS * d)


def _gated_delta_rule(q, k, v, g, beta):
    b, s, h, dk = q.shape
    dv = v.shape[-1]
    c = GDN_CHUNK
    q = _l2_normalize(q) * (dk ** -0.5)
    k = _l2_normalize(k)
    q, k, v, g, beta = (_to_chunks(t, c) for t in (q, k, v, g, beta))
    g_cum = jnp.cumsum(g, axis=-1)
    causal = jnp.tril(jnp.ones((c, c), dtype=bool))
    strict = jnp.tril(jnp.ones((c, c), dtype=bool), -1)
    decay = jnp.exp(jnp.where(causal, g_cum[..., :, None] - g_cum[..., None, :], -jnp.inf))
    kk = jnp.einsum('bhnid,bhnjd->bhnij', k, k)
    lower = jnp.where(strict, beta[..., :, None] * kk * decay, 0.0)
    a_mat = lower + jnp.eye(c, dtype=jnp.float32)
    rhs = jnp.concatenate([v * beta[..., None], k * (beta * jnp.exp(g_cum))[..., None]], axis=-1)
    sol = lax.linalg.triangular_solve(a_mat, rhs, left_side=True, lower=True, unit_diagonal=True)
    u, w = sol[..., :dv], sol[..., dv:]
    qk = jnp.einsum('bhnid,bhnjd->bhnij', q, k) * decay

    def step(state, xs):
        qc, kc, uc, wc, gc, ac = xs
        v_new = uc - jnp.einsum('bhck,bhkv->bhcv', wc, state)
        out = (jnp.einsum('bhck,bhkv->bhcv', qc * jnp.exp(gc)[..., None], state)
               + jnp.einsum('bhcs,bhsv->bhcv', ac, v_new))
        g_last = gc[..., -1:]
        state = (state * jnp.exp(g_last)[..., None]
                 + jnp.einsum('bhck,bhcv->bhkv', kc * jnp.exp(g_last - gc)[..., None], v_new))
        return state, out

    xs = tuple(jnp.moveaxis(t, 2, 0) for t in (q, k, u, w, g_cum, qk))
    state0 = jnp.zeros((b, h, dk, dv), jnp.float32)
    _, out = lax.scan(step, state0, xs)
    return _from_chunks(jnp.moveaxis(out, 0, 2))


def _multiscale_retention(q, k, v):
    b, s, h, dk = q.shape
    c = RET_CHUNK
    log_gamma = jnp.log(1.0 - 2.0 ** (-5.0 - jnp.arange(h, dtype=jnp.float32)))
    idx = jnp.arange(c, dtype=jnp.float32)
    rel = idx[:, None] - idx[None, :]
    d_mask = jnp.where(rel >= 0, jnp.exp(jnp.maximum(rel, 0.0)[None] * log_gamma[:, None, None]), 0.0)
    xi = jnp.exp((idx + 1.0)[None, :] * log_gamma[:, None])
    zeta = jnp.exp((c - 1.0 - idx)[None, :] * log_gamma[:, None])
    gamma_chunk = jnp.exp(c * log_gamma)
    k = k * (dk ** -0.5)
    q, k, v = (_to_chunks(t, c) for t in (q, k, v))
    scores = jnp.einsum('bhnid,bhnjd->bhnij', q, k) * d_mask[None, :, None]
    inner = jnp.einsum('bhnij,bhnjv->bhniv', scores, v)

    def step(state, xs):
        qc, kc, vc = xs
        cross = jnp.einsum('bhck,bhkv->bhcv', qc, state) * xi[None, :, :, None]
        state = (state * gamma_chunk[None, :, None, None]
                 + jnp.einsum('bhck,bhcv->bhkv', kc * zeta[None, :, :, None], vc))
        return state, cross

    xs = tuple(jnp.moveaxis(t, 2, 0) for t in (q, k, v))
    state0 = jnp.zeros((b, h, dk, v.shape[-1]), jnp.float32)
    _, cross = lax.scan(step, state0, xs)
    return _from_chunks(inner + jnp.moveaxis(cross, 0, 2))


def _mixers_jax(proj, ba, b, s, gdn_conv_w, gdn_a_log, gdn_dt_bias, gdn_norm_w, attn_sinks):
    f32 = jnp.float32
    p = proj.reshape(b, s, -1).astype(f32)
    pos = jnp.arange(s)
    a_q = p[..., A_OFF:A_OFF + SWA_WIDTH]
    a_k = p[..., A_OFF + SWA_WIDTH:A_OFF + SWA_WIDTH + SWA_KV_WIDTH]
    a_v = p[..., A_OFF + SWA_WIDTH + SWA_KV_WIDTH:B_OFF]
    qa = _rotary(a_q.reshape(b, s, SWA_Q_HEADS, HEAD_DIM), pos)
    ka = _rotary(a_k.reshape(b, s, SWA_KV_HEADS, HEAD_DIM), pos)
    va = a_v.reshape(b, s, SWA_KV_HEADS, HEAD_DIM)
    out_a = _sliding_window_attention(qa, ka, va, attn_sinks)

    b_qkv = p[..., B_OFF:B_OFF + 3 * GDN_WIDTH]
    b_z = p[..., B_OFF + 3 * GDN_WIDTH:C_OFF]
    bav = ba.reshape(b, s, -1)
    b_beta = bav[..., :GDN_HEADS]
    b_alpha = bav[..., GDN_HEADS:2 * GDN_HEADS]
    qkv = jax.nn.silu(_causal_depthwise_conv(b_qkv, gdn_conv_w))
    qb, kb, vb = jnp.split(qkv, 3, axis=-1)
    beta = jax.nn.sigmoid(b_beta)
    g = -jnp.exp(gdn_a_log.astype(f32)) * jax.nn.softplus(b_alpha + gdn_dt_bias.astype(f32))
    ob = _gated_delta_rule(qb.reshape(b, s, GDN_HEADS, HEAD_DIM), kb.reshape(b, s, GDN_HEADS, HEAD_DIM),
                           vb.reshape(b, s, GDN_HEADS, HEAD_DIM), g, beta)
    ob = ob * lax.rsqrt(jnp.mean(ob * ob, axis=-1, keepdims=True) + NORM_EPS) * gdn_norm_w.astype(f32)
    ob = ob * jax.nn.silu(b_z.reshape(b, s, GDN_HEADS, HEAD_DIM))
    out_b = ob.reshape(b, s, GDN_WIDTH)

    c_q = p[..., C_OFF:C_OFF + RET_WIDTH]
    c_k = p[..., C_OFF + RET_WIDTH:C_OFF + 2 * RET_WIDTH]
    c_v = p[..., C_OFF + 2 * RET_WIDTH:C_OFF + 3 * RET_WIDTH]
    c_g = p[..., C_OFF + 3 * RET_WIDTH:G_OFF]
    qc = _rotary(c_q.reshape(b, s, RET_HEADS, HEAD_DIM), pos)
    kc = _rotary(c_k.reshape(b, s, RET_HEADS, HEAD_DIM), pos)
    oc = _multiscale_retention(qc, kc, c_v.reshape(b, s, RET_HEADS, HEAD_DIM))
    mu = jnp.mean(oc, axis=-1, keepdims=True)
    var = jnp.mean(jnp.square(oc - mu), axis=-1, keepdims=True)
    oc = (oc - mu) * lax.rsqrt(var + NORM_EPS)
    out_c = jax.nn.silu(c_g) * oc.reshape(b, s, RET_WIDTH)
    bf = jnp.bfloat16
    t = b * s
    return (out_a.reshape(t, -1).astype(bf), out_b.reshape(t, -1).astype(bf), out_c.reshape(t, -1).astype(bf))


def kernel(x, ln_mix_w, ln_ffn_w, w_in, gdn_conv_w, gdn_a_log, gdn_dt_bias, gdn_norm_w, attn_sinks,
           w_branch, w_out, w_up, ffn_conv_w, ffn_conv_b, w_down, ln_final_w):
    b, s, d = x.shape
    t = b * s
    depth = w_in.shape[0]
    bf = jnp.bfloat16
    tm = min(1024, s)
    xf = x.reshape(t, d)
    for layer in range(depth):
        wi = w_in[layer]
        ba_lo = B_OFF + 4 * GDN_WIDTH
        w_in_p = jnp.concatenate([wi[:, :ba_lo], wi[:, ba_lo + 2 * GDN_HEADS:]], axis=1).astype(bf)
        w_ba = jnp.pad(wi[:, ba_lo:ba_lo + 2 * GDN_HEADS], ((0, 0), (0, BA_WIDTH - 2 * GDN_HEADS))).astype(bf)
        w_up_p = _interleave_gate_up(w_up[layer]).astype(bf)
        conv_w_p = _interleave_gate_up(ffn_conv_w[layer])
        conv_b_p = _interleave_gate_up(ffn_conv_b[layer]).reshape(1, -1)

        h = _rmsnorm(xf, ln_mix_w[layer], bf)
        proj = _matmul(h, w_in_p, bf, tm, 512, "in_proj")
        ba = _matmul(h, w_ba, jnp.float32, tm, BA_WIDTH, "in_proj_gates")
        out_a, out_b, out_c = _mixers_jax(proj, ba, b, s, gdn_conv_w[layer], gdn_a_log[layer],
                                          gdn_dt_bias[layer], gdn_norm_w[layer], attn_sinks[layer])
        merged = _branch_merge(out_a, out_b, out_c, w_branch[layer].astype(bf), proj, d, tm, 512)
        xf = _matmul_residual(merged, w_out[layer].astype(bf), xf, tm, 512, 1, "out_proj")

        h = _rmsnorm(xf, ln_ffn_w[layer], bf)
        act = _up_glu(h, w_up_p, conv_w_p, conv_b_p, s, tm)
        xf = _matmul_residual(act, w_down[layer].astype(bf), xf, tm, 512, 2, "down_proj")
    out = _rmsnorm(xf, ln_final_w, x.dtype)
    return out.reshape(b, s, d)
```

```python
import functools

import jax
import jax.numpy as jnp
from jax import lax
from jax.experimental import pallas as pl
from jax.experimental.pallas import tpu as pltpu

HEAD_DIM = 128
NORM_EPS = 1e-6
ROPE_THETA = 10000.0
SWA_Q_HEADS = 16
SWA_KV_HEADS = 4
SWA_GROUP = SWA_Q_HEADS // SWA_KV_HEADS
SWA_BLOCK = 128
GDN_HEADS = 8
GDN_CONV = 4
GDN_CHUNK = 64
GDN_ROWS = 256
GDN_HEADS_PER_STEP = 2
RET_HEADS = 8
RET_CHUNK = 128
FFN_CONV = 3
N_BRANCHES = 3

SWA_WIDTH = SWA_Q_HEADS * HEAD_DIM
SWA_KV_WIDTH = SWA_KV_HEADS * HEAD_DIM
GDN_WIDTH = GDN_HEADS * HEAD_DIM
RET_WIDTH = RET_HEADS * HEAD_DIM

A_OFF = 0
B_OFF = SWA_WIDTH + 2 * SWA_KV_WIDTH
C_OFF = B_OFF + 4 * GDN_WIDTH
G_OFF = C_OFF + 4 * RET_WIDTH
BA_WIDTH = 128

VMEM_LIMIT_BYTES = 52 * 1024 * 1024
FFN_TILE = 256
MASKED = -1e30

F32 = jnp.float32
BF16 = jnp.bfloat16
NT_DIMS = (((1,), (1,)), ((), ()))


def _cparams(*sem):
    return pltpu.CompilerParams(dimension_semantics=sem, vmem_limit_bytes=VMEM_LIMIT_BYTES)


def _mm(a, b):
    return jnp.dot(a.astype(BF16), b.astype(BF16), preferred_element_type=F32)


def _mm_nt(a, b):
    return lax.dot_general(a.astype(BF16), b.astype(BF16), NT_DIMS, preferred_element_type=F32)


def _silu(x):
    return x * jax.nn.sigmoid(x)


def _rope(x, cos, sin_signed):
    return x * cos + pltpu.roll(x, HEAD_DIM // 2, 1) * sin_signed


def _rmsnorm_kernel(x_ref, w_ref, o_ref):
    x = x_ref[...]
    y = x * lax.rsqrt(jnp.mean(x * x, axis=-1, keepdims=True) + NORM_EPS)
    o_ref[...] = (y * w_ref[...]).astype(o_ref.dtype)


def _rmsnorm(x, w, out_dtype, tm=256):
    t, d = x.shape
    return pl.pallas_call(
        _rmsnorm_kernel,
        grid=(t // tm,),
        in_specs=[pl.BlockSpec((tm, d), lambda i: (i, 0)),
                  pl.BlockSpec((1, d), lambda i: (0, 0))],
        out_specs=pl.BlockSpec((tm, d), lambda i: (i, 0)),
        out_shape=jax.ShapeDtypeStruct((t, d), out_dtype),
        compiler_params=_cparams("parallel"),
        name="rmsnorm",
    )(x, w.reshape(1, d))


def _mm_kernel(x_ref, w_ref, o_ref):
    o_ref[...] = jnp.dot(x_ref[...], w_ref[...],
                         preferred_element_type=F32).astype(o_ref.dtype)


def _matmul(x, w, out_dtype, tm, tn, name):
    m, k = x.shape
    n = w.shape[1]
    return pl.pallas_call(
        _mm_kernel,
        grid=(m // tm, n // tn),
        in_specs=[pl.BlockSpec((tm, k), lambda i, j: (i, 0)),
                  pl.BlockSpec((k, tn), lambda i, j: (0, j))],
        out_specs=pl.BlockSpec((tm, tn), lambda i, j: (i, j)),
        out_shape=jax.ShapeDtypeStruct((m, n), out_dtype),
        compiler_params=_cparams("parallel", "arbitrary"),
        name=name,
    )(x, w)


def _mm_res_kernel(x_ref, w_ref, r_ref, o_ref, acc_ref, *, nk):
    kk = pl.program_id(2)
    part = jnp.dot(x_ref[...], w_ref[...], preferred_element_type=F32)
    if nk == 1:
        o_ref[...] = r_ref[...] + part
    else:
        @pl.when(kk == 0)
        def _():
            acc_ref[...] = part

        @pl.when(jnp.logical_and(kk > 0, kk < nk - 1))
        def _():
            acc_ref[...] += part

        @pl.when(kk == nk - 1)
        def _():
            o_ref[...] = r_ref[...] + (acc_ref[...] + part)


def _matmul_residual(x, w, res, tm, tn, nk, name):
    m, k = x.shape
    n = w.shape[1]
    tk = k // nk
    return pl.pallas_call(
        functools.partial(_mm_res_kernel, nk=nk),
        grid=(m // tm, n // tn, nk),
        in_specs=[pl.BlockSpec((tm, tk), lambda i, j, kk: (i, kk)),
                  pl.BlockSpec((tk, tn), lambda i, j, kk: (kk, j)),
                  pl.BlockSpec((tm, tn), lambda i, j, kk: (i, j))],
        out_specs=pl.BlockSpec((tm, tn), lambda i, j, kk: (i, j)),
        out_shape=jax.ShapeDtypeStruct((m, n), F32),
        scratch_shapes=[pltpu.VMEM((tm, tn), F32)],
        compiler_params=_cparams("parallel", "arbitrary", "arbitrary"),
        name=name,
    )(x, w, res)


def _merge_kernel(a_ref, b_ref, c_ref, wa_ref, wb_ref, wc_ref, ga_ref, gb_ref, gc_ref, o_ref):
    def branch(x_ref, w_ref, g_ref):
        p = jnp.dot(x_ref[...], w_ref[...], preferred_element_type=F32)
        return jax.nn.sigmoid(g_ref[...].astype(F32)) * p

    merged = branch(a_ref, wa_ref, ga_ref) + branch(b_ref, wb_ref, gb_ref) + branch(c_ref, wc_ref, gc_ref)
    o_ref[...] = merged.astype(o_ref.dtype)


def _branch_merge(out_a, out_b, out_c, w_branch, proj, d_model, tm, tn):
    t = out_a.shape[0]
    gate_blk = G_OFF // tn
    d_blk = d_model // tn
    b_row = SWA_WIDTH // GDN_WIDTH
    return pl.pallas_call(
        _merge_kernel,
        grid=(t // tm, d_model // tn),
        in_specs=[pl.BlockSpec((tm, SWA_WIDTH), lambda i, j: (i, 0)),
                  pl.BlockSpec((tm, GDN_WIDTH), lambda i, j: (i, 0)),
                  pl.BlockSpec((tm, RET_WIDTH), lambda i, j: (i, 0)),
                  pl.BlockSpec((SWA_WIDTH, tn), lambda i, j: (0, j)),
                  pl.BlockSpec((GDN_WIDTH, tn), lambda i, j: (b_row, j)),
                  pl.BlockSpec((RET_WIDTH, tn), lambda i, j: (b_row + 1, j)),
                  pl.BlockSpec((tm, tn), lambda i, j: (i, gate_blk + j)),
                  pl.BlockSpec((tm, tn), lambda i, j: (i, gate_blk + d_blk + j)),
                  pl.BlockSpec((tm, tn), lambda i, j: (i, gate_blk + 2 * d_blk + j))],
        out_specs=pl.BlockSpec((tm, tn), lambda i, j: (i, j)),
        out_shape=jax.ShapeDtypeStruct((t, d_model), BF16),
        compiler_params=_cparams("parallel", "arbitrary"),
        name="branch_merge",
    )(out_a, out_b, out_c, w_branch, w_branch, w_branch, proj, proj, proj)


def _upglu_kernel(x_ref, w_ref, cw_ref, cb_ref, o_ref, u_ref, *, tm, tiles_per_seq):
    mi = pl.program_id(1)

    @pl.when(mi % tiles_per_seq == 0)
    def _():
        u_ref[0:8, :] = jnp.zeros((8, u_ref.shape[1]), F32)

    u_ref[8:8 + tm, :] = jnp.dot(x_ref[...], w_ref[...], preferred_element_type=F32)
    cw = cw_ref[...]
    y = cb_ref[...] + cw[2:3, :] * u_ref[8:8 + tm, :]
    y = y + cw[1:2, :] * u_ref[7:7 + tm, :]
    y = y + cw[0:1, :] * u_ref[6:6 + tm, :]
    u_ref[0:8, :] = u_ref[tm:tm + 8, :]
    gate = y[:, :FFN_TILE]
    up = y[:, FFN_TILE:]
    o_ref[...] = (_silu(gate) * up).astype(o_ref.dtype)


def _up_glu(h, w_up_p, conv_w_p, conv_b_p, seq, tm):
    t, d = h.shape
    n2 = w_up_p.shape[1]
    nblk = n2 // (2 * FFN_TILE)
    return pl.pallas_call(
        functools.partial(_upglu_kernel, tm=tm, tiles_per_seq=seq // tm),
        grid=(nblk, t // tm),
        in_specs=[pl.BlockSpec((tm, d), lambda j, i: (i, 0)),
                  pl.BlockSpec((d, 2 * FFN_TILE), lambda j, i: (0, j)),
                  pl.BlockSpec((FFN_CONV, 2 * FFN_TILE), lambda j, i: (0, j)),
                  pl.BlockSpec((1, 2 * FFN_TILE), lambda j, i: (0, j))],
        out_specs=pl.BlockSpec((tm, FFN_TILE), lambda j, i: (i, j)),
        out_shape=jax.ShapeDtypeStruct((t, nblk * FFN_TILE), BF16),
        scratch_shapes=[pltpu.VMEM((tm + 8, 2 * FFN_TILE), F32)],
        compiler_params=_cparams("parallel", "arbitrary"),
        name="up_conv_glu",
    )(h, w_up_p, conv_w_p, conv_b_p)


def _interleave_gate_up(a):
    lead = a.shape[:-1]
    f = a.shape[-1] // 2
    a = a.reshape(lead + (2, f // FFN_TILE, FFN_TILE))
    a = jnp.swapaxes(a, -3, -2)
    return a.reshape(lead + (2 * f,))


def _rope_tables(s):
    half = HEAD_DIM // 2
    inv_freq = ROPE_THETA ** (-jnp.arange(half, dtype=F32) / half)
    ang = jnp.arange(s, dtype=F32)[:, None] * inv_freq[None, :]
    cos, sin = jnp.cos(ang), jnp.sin(ang)
    return jnp.concatenate([cos, cos], axis=1), jnp.concatenate([-sin, sin], axis=1)


def _swa_kernel(sink_ref, q_ref, kc_ref, kp_ref, vc_ref, vp_ref, cos_ref, sin_ref, cosp_ref, sinp_ref, o_ref):
    n = pl.program_id(1)
    blk = SWA_BLOCK
    cos, sin = cos_ref[...], sin_ref[...]
    cosp, sinp = cosp_ref[...], sinp_ref[...]
    qi = lax.broadcasted_iota(jnp.int32, (blk, 2 * blk), 0)
    kj = lax.broadcasted_iota(jnp.int32, (blk, 2 * blk), 1)
    no_prev = jnp.where(n > 0, 0, blk)
    valid = jnp.where(kj < blk, kj - qi - no_prev, qi - kj + blk + 1) > 0
    scale = HEAD_DIM ** -0.5
    for hk in range(SWA_KV_HEADS):
        sl = slice(hk * HEAD_DIM, (hk + 1) * HEAD_DIM)
        kc = _rope(kc_ref[:, sl].astype(F32), cos, sin)
        kp = _rope(kp_ref[:, sl].astype(F32), cosp, sinp)
        k2 = jnp.concatenate([kp, kc], axis=0).astype(BF16)
        v2 = jnp.concatenate([vp_ref[:, sl], vc_ref[:, sl]], axis=0)
        qs = []
        for g in range(SWA_GROUP):
            hq = hk * SWA_GROUP + g
            qh = _rope(q_ref[:, hq * HEAD_DIM:(hq + 1) * HEAD_DIM].astype(F32), cos, sin) * scale
            qs.append(qh.astype(BF16))
        scores = _mm_nt(jnp.concatenate(qs, axis=0), k2)
        probs, inv_den = [], []
        for g in range(SWA_GROUP):
            sink = sink_ref[hk * SWA_GROUP + g]
            sg = jnp.where(valid, scores[g * blk:(g + 1) * blk], MASKED)
            m = jnp.maximum(jnp.max(sg, axis=1, keepdims=True), sink)
            p = jnp.exp(sg - m)
            den = jnp.sum(p, axis=1, keepdims=True) + jnp.exp(sink - m)
            probs.append(p.astype(BF16))
            inv_den.append(1.0 / den)
        out = jnp.dot(jnp.concatenate(probs, axis=0), v2, preferred_element_type=F32)
        for g in range(SWA_GROUP):
            hq = hk * SWA_GROUP + g
            o_ref[:, hq * HEAD_DIM:(hq + 1) * HEAD_DIM] = (out[g * blk:(g + 1) * blk] * inv_den[g]).astype(o_ref.dtype)


def _swa(proj, sinks, cos_t, sin_t, b, s):
    nb = s // SWA_BLOCK
    kcol = (A_OFF + SWA_WIDTH) // SWA_KV_WIDTH
    vcol = kcol + 1
    cur = lambda bi, n: (bi * nb + n, 0)
    prev_rows = lambda bi, n: bi * nb + jnp.maximum(n - 1, 0)
    return pl.pallas_call(
        _swa_kernel,
        grid=(b, nb),
        in_specs=[pl.BlockSpec(memory_space=pltpu.SMEM),
                  pl.BlockSpec((SWA_BLOCK, SWA_WIDTH), cur),
                  pl.BlockSpec((SWA_BLOCK, SWA_KV_WIDTH), lambda bi, n: (bi * nb + n, kcol)),
                  pl.BlockSpec((SWA_BLOCK, SWA_KV_WIDTH), lambda bi, n: (prev_rows(bi, n), kcol)),
                  pl.BlockSpec((SWA_BLOCK, SWA_KV_WIDTH), lambda bi, n: (bi * nb + n, vcol)),
                  pl.BlockSpec((SWA_BLOCK, SWA_KV_WIDTH), lambda bi, n: (prev_rows(bi, n), vcol)),
                  pl.BlockSpec((SWA_BLOCK, HEAD_DIM), lambda bi, n: (n, 0)),
                  pl.BlockSpec((SWA_BLOCK, HEAD_DIM), lambda bi, n: (n, 0)),
                  pl.BlockSpec((SWA_BLOCK, HEAD_DIM), lambda bi, n: (jnp.maximum(n - 1, 0), 0)),
                  pl.BlockSpec((SWA_BLOCK, HEAD_DIM), lambda bi, n: (jnp.maximum(n - 1, 0), 0))],
        out_specs=pl.BlockSpec((SWA_BLOCK, SWA_WIDTH), cur),
        out_shape=jax.ShapeDtypeStruct((b * s, SWA_WIDTH), BF16),
        compiler_params=_cparams("parallel", "parallel"),
        name="swa",
    )(sinks, proj, proj, proj, proj, proj, cos_t, sin_t, cos_t, sin_t)


def _ret_kernel(q_ref, k_ref, v_ref, g_ref, cos_ref, sin_ref, dm_ref, xi_ref, zeta_ref, gch_ref, o_ref, state_ref):
    @pl.when(pl.program_id(1) == 0)
    def _():
        state_ref[...] = jnp.zeros(state_ref.shape, F32)

    cos, sin = cos_ref[...], sin_ref[...]
    for h in range(RET_HEADS):
        sl = slice(h * HEAD_DIM, (h + 1) * HEAD_DIM)
        q = _rope(q_ref[:, sl].astype(F32), cos, sin)
        k = _rope(k_ref[:, sl].astype(F32), cos, sin) * (HEAD_DIM ** -0.5)
        v = v_ref[:, sl]
        scores = _mm_nt(q, k) * dm_ref[h]
        inner = _mm(scores, v)
        state = state_ref[h]
        cross = _mm(q, state) * xi_ref[h]
        kz_t = jnp.transpose(k * zeta_ref[h])
        state_ref[h] = state * gch_ref[h] + _mm(kz_t, v)
        o = inner + cross
        mu = jnp.mean(o, axis=-1, keepdims=True)
        var = jnp.mean(jnp.square(o - mu), axis=-1, keepdims=True)
        o = (o - mu) * lax.rsqrt(var + NORM_EPS)
        o_ref[:, sl] = (_silu(g_ref[:, sl].astype(F32)) * o).astype(o_ref.dtype)


def _retention(proj, cos_t, sin_t, b, s):
    c = RET_CHUNK
    nc = s // c
    h = RET_HEADS
    log_gamma = jnp.log(1.0 - 2.0 ** (-5.0 - jnp.arange(h, dtype=F32)))
    idx = jnp.arange(c, dtype=F32)
    rel = idx[:, None] - idx[None, :]
    d_mask = jnp.where(rel >= 0, jnp.exp(jnp.maximum(rel, 0.0)[None] * log_gamma[:, None, None]), 0.0)
    xi = jnp.exp((idx + 1.0)[None, :] * log_gamma[:, None])
    zeta = jnp.exp((c - 1.0 - idx)[None, :] * log_gamma[:, None])
    gamma_chunk = jnp.exp(c * log_gamma)
    xi_b = jnp.broadcast_to(xi[:, :, None], (h, c, HEAD_DIM))
    zeta_b = jnp.broadcast_to(zeta[:, :, None], (h, c, HEAD_DIM))
    gch_b = jnp.broadcast_to(gamma_chunk[:, None, None], (h, 1, HEAD_DIM))
    col0 = C_OFF // RET_WIDTH
    whole3 = lambda bi, n: (0, 0, 0)
    row = lambda bi, n: bi * nc + n
    return pl.pallas_call(
        _ret_kernel,
        grid=(b, nc),
        in_specs=[pl.BlockSpec((c, RET_WIDTH), lambda bi, n: (row(bi, n), col0)),
                  pl.BlockSpec((c, RET_WIDTH), lambda bi, n: (row(bi, n), col0 + 1)),
                  pl.BlockSpec((c, RET_WIDTH), lambda bi, n: (row(bi, n), col0 + 2)),
                  pl.BlockSpec((c, RET_WIDTH), lambda bi, n: (row(bi, n), col0 + 3)),
                  pl.BlockSpec((c, HEAD_DIM), lambda bi, n: (n, 0)),
                  pl.BlockSpec((c, HEAD_DIM), lambda bi, n: (n, 0)),
                  pl.BlockSpec((h, c, c), whole3),
                  pl.BlockSpec((h, c, HEAD_DIM), whole3),
                  pl.BlockSpec((h, c, HEAD_DIM), whole3),
                  pl.BlockSpec((h, 1, HEAD_DIM), whole3)],
        out_specs=pl.BlockSpec((c, RET_WIDTH), lambda bi, n: (row(bi, n), 0)),
        out_shape=jax.ShapeDtypeStruct((b * s, RET_WIDTH), BF16),
        scratch_shapes=[pltpu.VMEM((h, HEAD_DIM, HEAD_DIM), F32)],
        compiler_params=_cparams("parallel", "arbitrary"),
        name="retention",
    )(proj, proj, proj, proj, cos_t, sin_t, d_mask, xi_b, zeta_b, gch_b)


def _softplus(x):
    return jnp.maximum(x, 0.0) + jnp.log(1.0 + jnp.exp(-jnp.abs(x)))


def _gdn_kernel(alog_ref, dtb_ref, q_ref, k_ref, v_ref, z_ref, cwq_ref, cwk_ref, cwv_ref, ba_ref, bat_ref, nw_ref,
                o_ref, cbuf_ref, state_ref, *, heads):
    hg = pl.program_id(1)
    n = pl.program_id(2)
    rows, chunk, d = GDN_ROWS, GDN_CHUNK, HEAD_DIM
    hw = heads * d

    @pl.when(n == 0)
    def _():
        state_ref[...] = jnp.zeros(state_ref.shape, F32)
        cbuf_ref[0:8, :] = jnp.zeros((8, cbuf_ref.shape[1]), F32)

    for i, (x_ref, cw_ref) in enumerate(((q_ref, cwq_ref), (k_ref, cwk_ref), (v_ref, cwv_ref))):
        cs = slice(i * hw, (i + 1) * hw)
        cbuf_ref[8:8 + rows, cs] = x_ref[...].astype(F32)
        cw = cw_ref[...]
        y = cw[3:4, :] * cbuf_ref[8:8 + rows, cs]
        for tap in range(1, GDN_CONV):
            y = y + cw[3 - tap:4 - tap, :] * cbuf_ref[8 - tap:8 - tap + rows, cs]
        cbuf_ref[0:8, cs] = cbuf_ref[rows:rows + 8, cs]
        cbuf_ref[8:8 + rows, cs] = _silu(y)

    ri = lax.broadcasted_iota(jnp.int32, (rows, rows), 0)
    ci = lax.broadcasted_iota(jnp.int32, (rows, rows), 1)

    def same(blk):
        sh = blk.bit_length() - 1
        return jnp.right_shift(ri, sh) == jnp.right_shift(ci, sh)

    causal = jnp.logical_and(same(chunk), ci <= ri)
    causal_t = jnp.logical_and(same(chunk), ri <= ci)
    strict = jnp.logical_and(same(chunk), ci < ri)
    lane = lax.broadcasted_iota(jnp.int32, (rows, BA_WIDTH), 1)
    ba = ba_ref[...]

    for hh in range(heads):
        head = hg * heads + hh
        hs = slice(hh * d, (hh + 1) * d)
        q = cbuf_ref[8:8 + rows, hh * d:(hh + 1) * d]
        k = cbuf_ref[8:8 + rows, hw + hh * d:hw + (hh + 1) * d]
        v = cbuf_ref[8:8 + rows, 2 * hw + hh * d:2 * hw + (hh + 1) * d]
        q = q * lax.rsqrt(jnp.sum(q * q, axis=-1, keepdims=True) + NORM_EPS) * (d ** -0.5)
        k = k * lax.rsqrt(jnp.sum(k * k, axis=-1, keepdims=True) + NORM_EPS)

        a_neg = -jnp.exp(alog_ref[head])
        dtb = dtb_ref[head]
        beta_c = jax.nn.sigmoid(jnp.sum(jnp.where(lane == head, ba, 0.0), axis=1, keepdims=True))
        alpha_c = jnp.sum(jnp.where(lane == GDN_HEADS + head, ba, 0.0), axis=1, keepdims=True)
        alpha_r = bat_ref[pl.ds(GDN_HEADS + head, 1), :]
        g_c = a_neg * _softplus(alpha_c + dtb)
        g_r = a_neg * _softplus(alpha_r + dtb)
        gc_c = jnp.sum(jnp.where(causal, g_r, 0.0), axis=1, keepdims=True)
        gc_r = jnp.sum(jnp.where(causal_t, g_c, 0.0), axis=0, keepdims=True)
        decay = jnp.exp(jnp.where(causal, gc_c - gc_r, MASKED))

        kb = k.astype(BF16)
        low = jnp.where(strict, beta_c * _mm_nt(kb, kb) * decay, 0.0)
        dg = jnp.where(same(16), low, 0.0)
        d2 = _mm(dg, dg)
        d4 = _mm(d2, d2)
        d8 = _mm(d4, d4)
        e1 = d2 - dg - _mm(dg, d2)
        e2 = d4 + d8 + _mm(d4, d8)
        e = e1 + e2 + _mm(e1, e2)
        for blk in (16, 32):
            off = jnp.where(jnp.logical_and(same(2 * blk), jnp.logical_not(same(blk))), low, 0.0)
            y = off + _mm(off, e)
            e = e - y - _mm(e, y)

        rhs = jnp.concatenate([v * beta_c, k * (beta_c * jnp.exp(gc_c))], axis=1)
        sol = rhs + _mm(e, rhs)
        u, w = sol[:, :d], sol[:, d:]
        qk = _mm_nt(q, kb) * decay
        qg = q * jnp.exp(gc_c)

        state = state_ref[hh]
        v_new, cross = [], []
        for c in range(rows // chunk):
            r0 = c * chunk
            rs = slice(r0, r0 + chunk)
            both = _mm(jnp.concatenate([w[rs], qg[rs]], axis=0), state)
            vn = u[rs] - both[:chunk]
            cross.append(both[chunk:])
            g_last = gc_c[r0 + chunk - 1:r0 + chunk, :]
            kd_t = jnp.transpose(k[rs] * jnp.exp(g_last - gc_c[rs]))
            state = state * jnp.exp(g_last) + _mm(kd_t, vn)
            v_new.append(vn)
        state_ref[hh] = state
        o = jnp.concatenate(cross, axis=0) + _mm(qk, jnp.concatenate(v_new, axis=0))
        o = o * lax.rsqrt(jnp.mean(o * o, axis=-1, keepdims=True) + NORM_EPS) * nw_ref[...]
        o_ref[:, hs] = (o * _silu(z_ref[:, hs].astype(F32))).astype(o_ref.dtype)


def _gdn(proj, ba, conv_w, a_log, dt_bias, norm_w, b, s):
    rows, heads = GDN_ROWS, GDN_HEADS_PER_STEP
    nt = s // rows
    hw = heads * HEAD_DIM
    groups = GDN_HEADS // heads
    col0 = B_OFF // hw
    gcol = GDN_WIDTH // hw
    row = lambda bi, hg, n: bi * nt + n
    smem = pl.BlockSpec(memory_space=pltpu.SMEM)

    def xspec(i):
        return pl.BlockSpec((rows, hw), lambda bi, hg, n: (row(bi, hg, n), col0 + i * gcol + hg))

    def cwspec(i):
        return pl.BlockSpec((GDN_CONV, hw), lambda bi, hg, n: (0, i * gcol + hg))

    return pl.pallas_call(
        functools.partial(_gdn_kernel, heads=heads),
        grid=(b, groups, nt),
        in_specs=[smem, smem, xspec(0), xspec(1), xspec(2), xspec(3), cwspec(0), cwspec(1), cwspec(2),
                  pl.BlockSpec((rows, BA_WIDTH), lambda bi, hg, n: (row(bi, hg, n), 0)),
                  pl.BlockSpec((2 * GDN_HEADS, rows), lambda bi, hg, n: (0, row(bi, hg, n))),
                  pl.BlockSpec((1, HEAD_DIM), lambda bi, hg, n: (0, 0))],
        out_specs=pl.BlockSpec((rows, hw), lambda bi, hg, n: (row(bi, hg, n), hg)),
        out_shape=jax.ShapeDtypeStruct((b * s, GDN_WIDTH), BF16),
        scratch_shapes=[pltpu.VMEM((rows + 8, 3 * hw), F32),
                        pltpu.VMEM((heads, HEAD_DIM, HEAD_DIM), F32)],
        compiler_params=_cparams("parallel", "parallel", "arbitrary"),
        name="gdn",
    )(a_log, dt_bias, proj, proj, proj, proj, conv_w, conv_w, conv_w, ba,
      jnp.transpose(ba[:, :2 * GDN_HEADS]), norm_w.reshape(1, HEAD_DIM))


def kernel(x, ln_mix_w, ln_ffn_w, w_in, gdn_conv_w, gdn_a_log, gdn_dt_bias, gdn_norm_w, attn_sinks,
           w_branch, w_out, w_up, ffn_conv_w, ffn_conv_b, w_down, ln_final_w):
    b, s, d = x.shape
    t = b * s
    depth = w_in.shape[0]
    tm = min(1024, s)
    xf = x.reshape(t, d)
    cos_t, sin_t = _rope_tables(s)
    for layer in range(depth):
        wi = w_in[layer]
        ba_lo = B_OFF + 4 * GDN_WIDTH
        w_in_p = jnp.concatenate([wi[:, :ba_lo], wi[:, ba_lo + 2 * GDN_HEADS:]], axis=1).astype(BF16)
        w_ba = jnp.pad(wi[:, ba_lo:ba_lo + 2 * GDN_HEADS], ((0, 0), (0, BA_WIDTH - 2 * GDN_HEADS))).astype(BF16)
        w_up_p = _interleave_gate_up(w_up[layer]).astype(BF16)
        conv_w_p = _interleave_gate_up(ffn_conv_w[layer])
        conv_b_p = _interleave_gate_up(ffn_conv_b[layer]).reshape(1, -1)

        h = _rmsnorm(xf, ln_mix_w[layer], BF16)
        proj = _matmul(h, w_in_p, BF16, tm, 512, "in_proj")
        ba = _matmul(h, w_ba, F32, tm, BA_WIDTH, "in_proj_gates")
        out_a = _swa(proj, attn_sinks[layer], cos_t, sin_t, b, s)
        out_b = _gdn(proj, ba, gdn_conv_w[layer], gdn_a_log[layer], gdn_dt_bias[layer], gdn_norm_w[layer], b, s)
        out_c = _retention(proj, cos_t, sin_t, b, s)
        merged = _branch_merge(out_a, out_b, out_c, w_branch[layer].astype(BF16), proj, d, tm, 512)
        xf = _matmul_residual(merged, w_out[layer].astype(BF16), xf, tm, 512, 1, "out_proj")

        h = _rmsnorm(xf, ln_ffn_w[layer], BF16)
        act = _up_glu(h, w_up_p, conv_w_p, conv_b_p, s, tm)
        xf = _matmul_residual(act, w_down[layer].astype(BF16), xf, tm, 512, 2, "down_proj")
    out = _rmsnorm(xf, ln_final_w, x.dtype)
    return out.reshape(b, s, d)
```

```python
import functools

import jax
import jax.numpy as jnp
from jax import lax
from jax.experimental import pallas as pl
from jax.experimental.pallas import tpu as pltpu

HEAD_DIM = 128
NORM_EPS = 1e-6
ROPE_THETA = 10000.0
SWA_Q_HEADS = 16
SWA_KV_HEADS = 4
SWA_GROUP = SWA_Q_HEADS // SWA_KV_HEADS
SWA_BLOCK = 128
GDN_HEADS = 8
GDN_CONV = 4
GDN_CHUNK = 64
GDN_ROWS = 256
GDN_HEADS_PER_STEP = 4
RET_HEADS = 8
RET_CHUNK = 128
FFN_CONV = 3
N_BRANCHES = 3

SWA_WIDTH = SWA_Q_HEADS * HEAD_DIM
SWA_KV_WIDTH = SWA_KV_HEADS * HEAD_DIM
GDN_WIDTH = GDN_HEADS * HEAD_DIM
RET_WIDTH = RET_HEADS * HEAD_DIM

A_OFF = 0
B_OFF = SWA_WIDTH + 2 * SWA_KV_WIDTH
C_OFF = B_OFF + 4 * GDN_WIDTH
G_OFF = C_OFF + 4 * RET_WIDTH
BA_WIDTH = 128

VMEM_LIMIT_BYTES = 52 * 1024 * 1024
FFN_TILE = 256
UP_SUB_ROWS = 256
MASKED = -1e30

F32 = jnp.float32
BF16 = jnp.bfloat16
NT_DIMS = (((1,), (1,)), ((), ()))


def _cparams(*sem):
    return pltpu.CompilerParams(dimension_semantics=sem, vmem_limit_bytes=VMEM_LIMIT_BYTES)


def _mm(a, b):
    return jnp.dot(a.astype(BF16), b.astype(BF16), preferred_element_type=F32)


def _mm_nt(a, b):
    return lax.dot_general(a.astype(BF16), b.astype(BF16), NT_DIMS, preferred_element_type=F32)


def _silu(x):
    return x * jax.nn.sigmoid(x)


def _rope(x, cos, sin_signed):
    return x * cos + pltpu.roll(x, HEAD_DIM // 2, 1) * sin_signed


def _rmsnorm_kernel(x_ref, w_ref, o_ref):
    x = x_ref[...]
    y = x * lax.rsqrt(jnp.mean(x * x, axis=-1, keepdims=True) + NORM_EPS)
    o_ref[...] = (y * w_ref[...]).astype(o_ref.dtype)


def _rmsnorm(x, w, out_dtype, tm=256):
    t, d = x.shape
    return pl.pallas_call(
        _rmsnorm_kernel,
        grid=(t // tm,),
        in_specs=[pl.BlockSpec((tm, d), lambda i: (i, 0)),
                  pl.BlockSpec((1, d), lambda i: (0, 0))],
        out_specs=pl.BlockSpec((tm, d), lambda i: (i, 0)),
        out_shape=jax.ShapeDtypeStruct((t, d), out_dtype),
        compiler_params=_cparams("parallel"),
        name="rmsnorm",
    )(x, w.reshape(1, d))


def _mm_kernel(x_ref, w_ref, o_ref):
    o_ref[...] = jnp.dot(x_ref[...], w_ref[...],
                         preferred_element_type=F32).astype(o_ref.dtype)


def _matmul(x, w, out_dtype, tm, tn, name):
    m, k = x.shape
    n = w.shape[1]
    return pl.pallas_call(
        _mm_kernel,
        grid=(m // tm, n // tn),
        in_specs=[pl.BlockSpec((tm, k), lambda i, j: (i, 0)),
                  pl.BlockSpec((k, tn), lambda i, j: (0, j))],
        out_specs=pl.BlockSpec((tm, tn), lambda i, j: (i, j)),
        out_shape=jax.ShapeDtypeStruct((m, n), out_dtype),
        compiler_params=_cparams("parallel", "arbitrary"),
        name=name,
    )(x, w)


def _mm_res_kernel(x_ref, w_ref, r_ref, o_ref, acc_ref, *, nk):
    kk = pl.program_id(2)
    part = jnp.dot(x_ref[...], w_ref[...], preferred_element_type=F32)
    if nk == 1:
        o_ref[...] = r_ref[...] + part
    else:
        @pl.when(kk == 0)
        def _():
            acc_ref[...] = part

        @pl.when(jnp.logical_and(kk > 0, kk < nk - 1))
        def _():
            acc_ref[...] += part

        @pl.when(kk == nk - 1)
        def _():
            o_ref[...] = r_ref[...] + (acc_ref[...] + part)


def _matmul_residual(x, w, res, tm, tn, nk, name):
    m, k = x.shape
    n = w.shape[1]
    tk = k // nk
    return pl.pallas_call(
        functools.partial(_mm_res_kernel, nk=nk),
        grid=(m // tm, n // tn, nk),
        in_specs=[pl.BlockSpec((tm, tk), lambda i, j, kk: (i, kk)),
                  pl.BlockSpec((tk, tn), lambda i, j, kk: (kk, j)),
                  pl.BlockSpec((tm, tn), lambda i, j, kk: (i, j))],
        out_specs=pl.BlockSpec((tm, tn), lambda i, j, kk: (i, j)),
        out_shape=jax.ShapeDtypeStruct((m, n), F32),
        scratch_shapes=[pltpu.VMEM((tm, tn), F32)],
        compiler_params=_cparams("parallel", "arbitrary", "arbitrary"),
        name=name,
    )(x, w, res)


def _merge_kernel(a_ref, b_ref, c_ref, wa_ref, wb_ref, wc_ref, ga_ref, gb_ref, gc_ref, o_ref):
    def branch(x_ref, w_ref, g_ref):
        p = jnp.dot(x_ref[...], w_ref[...], preferred_element_type=F32)
        return jax.nn.sigmoid(g_ref[...].astype(F32)) * p

    merged = branch(a_ref, wa_ref, ga_ref) + branch(b_ref, wb_ref, gb_ref) + branch(c_ref, wc_ref, gc_ref)
    o_ref[...] = merged.astype(o_ref.dtype)


def _branch_merge(out_a, out_b, out_c, w_branch, proj, d_model, tm, tn):
    t = out_a.shape[0]
    gate_blk = G_OFF // tn
    d_blk = d_model // tn
    b_row = SWA_WIDTH // GDN_WIDTH
    return pl.pallas_call(
        _merge_kernel,
        grid=(t // tm, d_model // tn),
        in_specs=[pl.BlockSpec((tm, SWA_WIDTH), lambda i, j: (i, 0)),
                  pl.BlockSpec((tm, GDN_WIDTH), lambda i, j: (i, 0)),
                  pl.BlockSpec((tm, RET_WIDTH), lambda i, j: (i, 0)),
                  pl.BlockSpec((SWA_WIDTH, tn), lambda i, j: (0, j)),
                  pl.BlockSpec((GDN_WIDTH, tn), lambda i, j: (b_row, j)),
                  pl.BlockSpec((RET_WIDTH, tn), lambda i, j: (b_row + 1, j)),
                  pl.BlockSpec((tm, tn), lambda i, j: (i, gate_blk + j)),
                  pl.BlockSpec((tm, tn), lambda i, j: (i, gate_blk + d_blk + j)),
                  pl.BlockSpec((tm, tn), lambda i, j: (i, gate_blk + 2 * d_blk + j))],
        out_specs=pl.BlockSpec((tm, tn), lambda i, j: (i, j)),
        out_shape=jax.ShapeDtypeStruct((t, d_model), BF16),
        compiler_params=_cparams("parallel", "arbitrary"),
        name="branch_merge",
    )(out_a, out_b, out_c, w_branch, w_branch, w_branch, proj, proj, proj)


def _upglu_kernel(x_ref, w_ref, cw_ref, cb_ref, o_ref, u_ref, *, tm, tiles_per_seq):
    mi = pl.program_id(1)

    @pl.when(mi % tiles_per_seq == 0)
    def _():
        u_ref[0:8, :] = jnp.zeros((8, u_ref.shape[1]), F32)

    cw = cw_ref[...]
    sub = min(tm, UP_SUB_ROWS)
    for r0 in range(0, tm, sub):
        u_ref[8 + r0:8 + r0 + sub, :] = jnp.dot(x_ref[r0:r0 + sub, :], w_ref[...], preferred_element_type=F32)
        y = cb_ref[...] + cw[2:3, :] * u_ref[8 + r0:8 + r0 + sub, :]
        y = y + cw[1:2, :] * u_ref[7 + r0:7 + r0 + sub, :]
        y = y + cw[0:1, :] * u_ref[6 + r0:6 + r0 + sub, :]
        o_ref[r0:r0 + sub, :] = (_silu(y[:, :FFN_TILE]) * y[:, FFN_TILE:]).astype(o_ref.dtype)
    u_ref[0:8, :] = u_ref[tm:tm + 8, :]


def _up_glu(h, w_up_p, conv_w_p, conv_b_p, seq, tm):
    t, d = h.shape
    n2 = w_up_p.shape[1]
    nblk = n2 // (2 * FFN_TILE)
    return pl.pallas_call(
        functools.partial(_upglu_kernel, tm=tm, tiles_per_seq=seq // tm),
        grid=(nblk, t // tm),
        in_specs=[pl.BlockSpec((tm, d), lambda j, i: (i, 0)),
                  pl.BlockSpec((d, 2 * FFN_TILE), lambda j, i: (0, j)),
                  pl.BlockSpec((FFN_CONV, 2 * FFN_TILE), lambda j, i: (0, j)),
                  pl.BlockSpec((1, 2 * FFN_TILE), lambda j, i: (0, j))],
        out_specs=pl.BlockSpec((tm, FFN_TILE), lambda j, i: (i, j)),
        out_shape=jax.ShapeDtypeStruct((t, nblk * FFN_TILE), BF16),
        scratch_shapes=[pltpu.VMEM((tm + 8, 2 * FFN_TILE), F32)],
        compiler_params=_cparams("parallel", "arbitrary"),
        name="up_conv_glu",
    )(h, w_up_p, conv_w_p, conv_b_p)


def _cast_kernel(w_ref, o_ref):
    o_ref[...] = w_ref[...].astype(o_ref.dtype)


def _cast_bf16(w, tk):
    k, n = w.shape
    return pl.pallas_call(
        _cast_kernel,
        grid=(k // tk,),
        in_specs=[pl.BlockSpec((tk, n), lambda i: (i, 0))],
        out_specs=pl.BlockSpec((tk, n), lambda i: (i, 0)),
        out_shape=jax.ShapeDtypeStruct((k, n), BF16),
        compiler_params=_cparams("parallel"),
        name="cast_bf16",
    )(w)


def _cast_up_kernel(g_ref, u_ref, o_ref):
    o_ref[:, :FFN_TILE] = g_ref[...].astype(o_ref.dtype)
    o_ref[:, FFN_TILE:] = u_ref[...].astype(o_ref.dtype)


def _cast_interleave_up(w_up):
    k, n2 = w_up.shape
    nblk = n2 // (2 * FFN_TILE)
    return pl.pallas_call(
        _cast_up_kernel,
        grid=(nblk,),
        in_specs=[pl.BlockSpec((k, FFN_TILE), lambda j: (0, j)),
                  pl.BlockSpec((k, FFN_TILE), lambda j: (0, nblk + j))],
        out_specs=pl.BlockSpec((k, 2 * FFN_TILE), lambda j: (0, j)),
        out_shape=jax.ShapeDtypeStruct((k, n2), BF16),
        compiler_params=_cparams("parallel"),
        name="cast_up",
    )(w_up, w_up)


IN_PACK_TILE = 512
IN_PACK_SHIFT = 2 * GDN_HEADS


def _cast_in_kernel(a_ref, b_ref, o_ref, *, first_shifted):
    j = pl.program_id(1)

    @pl.when(j < first_shifted)
    def _():
        o_ref[...] = a_ref[...].astype(o_ref.dtype)

    @pl.when(j >= first_shifted)
    def _():
        shifted = jnp.concatenate([a_ref[:, IN_PACK_SHIFT:], b_ref[:, :IN_PACK_SHIFT]], axis=1)
        o_ref[...] = shifted.astype(o_ref.dtype)


def _cast_pack_in(w_in, tk):
    k, n = w_in.shape
    ba_lo = B_OFF + 4 * GDN_WIDTH
    n_packed = n - IN_PACK_SHIFT
    tn = IN_PACK_TILE
    lanes = HEAD_DIM
    return pl.pallas_call(
        functools.partial(_cast_in_kernel, first_shifted=ba_lo // tn),
        grid=(k // tk, n_packed // tn),
        in_specs=[pl.BlockSpec((tk, tn), lambda i, j: (i, j)),
                  pl.BlockSpec((tk, lanes), lambda i, j: (i, (tn // lanes) * (j + 1)))],
        out_specs=pl.BlockSpec((tk, tn), lambda i, j: (i, j)),
        out_shape=jax.ShapeDtypeStruct((k, n_packed), BF16),
        compiler_params=_cparams("parallel", "parallel"),
        name="cast_in",
    )(w_in, w_in)


def _interleave_gate_up(a):
    lead = a.shape[:-1]
    f = a.shape[-1] // 2
    a = a.reshape(lead + (2, f // FFN_TILE, FFN_TILE))
    a = jnp.swapaxes(a, -3, -2)
    return a.reshape(lead + (2 * f,))


def _rope_tables(s):
    half = HEAD_DIM // 2
    inv_freq = ROPE_THETA ** (-jnp.arange(half, dtype=F32) / half)
    ang = jnp.arange(s, dtype=F32)[:, None] * inv_freq[None, :]
    cos, sin = jnp.cos(ang), jnp.sin(ang)
    return jnp.concatenate([cos, cos], axis=1), jnp.concatenate([-sin, sin], axis=1)


def _swa_kernel(sink_ref, q_ref, kc_ref, kp_ref, vc_ref, vp_ref, cos_ref, sin_ref, cosp_ref, sinp_ref, o_ref):
    n = pl.program_id(1)
    blk = SWA_BLOCK
    cos, sin = cos_ref[...], sin_ref[...]
    cosp, sinp = cosp_ref[...], sinp_ref[...]
    qi = lax.broadcasted_iota(jnp.int32, (blk, 2 * blk), 0)
    kj = lax.broadcasted_iota(jnp.int32, (blk, 2 * blk), 1)
    no_prev = jnp.where(n > 0, 0, blk)
    valid = jnp.where(kj < blk, kj - qi - no_prev, qi - kj + blk + 1) > 0
    scale = HEAD_DIM ** -0.5
    for hk in range(SWA_KV_HEADS):
        sl = slice(hk * HEAD_DIM, (hk + 1) * HEAD_DIM)
        kc = _rope(kc_ref[:, sl].astype(F32), cos, sin)
        kp = _rope(kp_ref[:, sl].astype(F32), cosp, sinp)
        k2 = jnp.concatenate([kp, kc], axis=0).astype(BF16)
        v2 = jnp.concatenate([vp_ref[:, sl], vc_ref[:, sl]], axis=0)
        qs = []
        for g in range(SWA_GROUP):
            hq = hk * SWA_GROUP + g
            qh = _rope(q_ref[:, hq * HEAD_DIM:(hq + 1) * HEAD_DIM].astype(F32), cos, sin) * scale
            qs.append(qh.astype(BF16))
        scores = _mm_nt(jnp.concatenate(qs, axis=0), k2)
        probs, inv_den = [], []
        for g in range(SWA_GROUP):
            sink = sink_ref[hk * SWA_GROUP + g]
            sg = jnp.where(valid, scores[g * blk:(g + 1) * blk], MASKED)
            m = jnp.maximum(jnp.max(sg, axis=1, keepdims=True), sink)
            p = jnp.exp(sg - m)
            den = jnp.sum(p, axis=1, keepdims=True) + jnp.exp(sink - m)
            probs.append(p.astype(BF16))
            inv_den.append(1.0 / den)
        out = jnp.dot(jnp.concatenate(probs, axis=0), v2, preferred_element_type=F32)
        for g in range(SWA_GROUP):
            hq = hk * SWA_GROUP + g
            o_ref[:, hq * HEAD_DIM:(hq + 1) * HEAD_DIM] = (out[g * blk:(g + 1) * blk] * inv_den[g]).astype(o_ref.dtype)


def _swa(proj, sinks, cos_t, sin_t, b, s):
    nb = s // SWA_BLOCK
    kcol = (A_OFF + SWA_WIDTH) // SWA_KV_WIDTH
    vcol = kcol + 1
    cur = lambda bi, n: (bi * nb + n, 0)
    prev_rows = lambda bi, n: bi * nb + jnp.maximum(n - 1, 0)
    return pl.pallas_call(
        _swa_kernel,
        grid=(b, nb),
        in_specs=[pl.BlockSpec(memory_space=pltpu.SMEM),
                  pl.BlockSpec((SWA_BLOCK, SWA_WIDTH), cur),
                  pl.BlockSpec((SWA_BLOCK, SWA_KV_WIDTH), lambda bi, n: (bi * nb + n, kcol)),
                  pl.BlockSpec((SWA_BLOCK, SWA_KV_WIDTH), lambda bi, n: (prev_rows(bi, n), kcol)),
                  pl.BlockSpec((SWA_BLOCK, SWA_KV_WIDTH), lambda bi, n: (bi * nb + n, vcol)),
                  pl.BlockSpec((SWA_BLOCK, SWA_KV_WIDTH), lambda bi, n: (prev_rows(bi, n), vcol)),
                  pl.BlockSpec((SWA_BLOCK, HEAD_DIM), lambda bi, n: (n, 0)),
                  pl.BlockSpec((SWA_BLOCK, HEAD_DIM), lambda bi, n: (n, 0)),
                  pl.BlockSpec((SWA_BLOCK, HEAD_DIM), lambda bi, n: (jnp.maximum(n - 1, 0), 0)),
                  pl.BlockSpec((SWA_BLOCK, HEAD_DIM), lambda bi, n: (jnp.maximum(n - 1, 0), 0))],
        out_specs=pl.BlockSpec((SWA_BLOCK, SWA_WIDTH), cur),
        out_shape=jax.ShapeDtypeStruct((b * s, SWA_WIDTH), BF16),
        compiler_params=_cparams("parallel", "parallel"),
        name="swa",
    )(sinks, proj, proj, proj, proj, proj, cos_t, sin_t, cos_t, sin_t)


def _ret_kernel(q_ref, k_ref, v_ref, g_ref, cos_ref, sin_ref, dm_ref, xi_ref, zeta_ref, gch_ref, o_ref, state_ref):
    @pl.when(pl.program_id(1) == 0)
    def _():
        state_ref[...] = jnp.zeros(state_ref.shape, F32)

    cos, sin = cos_ref[...], sin_ref[...]
    for h in range(RET_HEADS):
        sl = slice(h * HEAD_DIM, (h + 1) * HEAD_DIM)
        q = _rope(q_ref[:, sl].astype(F32), cos, sin)
        k = _rope(k_ref[:, sl].astype(F32), cos, sin) * (HEAD_DIM ** -0.5)
        v = v_ref[:, sl]
        scores = _mm_nt(q, k) * dm_ref[h]
        inner = _mm(scores, v)
        state = state_ref[h]
        cross = _mm(q, state) * xi_ref[h]
        kz_t = jnp.transpose(k * zeta_ref[h])
        state_ref[h] = state * gch_ref[h] + _mm(kz_t, v)
        o = inner + cross
        mu = jnp.mean(o, axis=-1, keepdims=True)
        var = jnp.mean(jnp.square(o - mu), axis=-1, keepdims=True)
        o = (o - mu) * lax.rsqrt(var + NORM_EPS)
        o_ref[:, sl] = (_silu(g_ref[:, sl].astype(F32)) * o).astype(o_ref.dtype)


def _retention(proj, cos_t, sin_t, b, s):
    c = RET_CHUNK
    nc = s // c
    h = RET_HEADS
    log_gamma = jnp.log(1.0 - 2.0 ** (-5.0 - jnp.arange(h, dtype=F32)))
    idx = jnp.arange(c, dtype=F32)
    rel = idx[:, None] - idx[None, :]
    d_mask = jnp.where(rel >= 0, jnp.exp(jnp.maximum(rel, 0.0)[None] * log_gamma[:, None, None]), 0.0)
    xi = jnp.exp((idx + 1.0)[None, :] * log_gamma[:, None])
    zeta = jnp.exp((c - 1.0 - idx)[None, :] * log_gamma[:, None])
    gamma_chunk = jnp.exp(c * log_gamma)
    xi_b = jnp.broadcast_to(xi[:, :, None], (h, c, HEAD_DIM))
    zeta_b = jnp.broadcast_to(zeta[:, :, None], (h, c, HEAD_DIM))
    gch_b = jnp.broadcast_to(gamma_chunk[:, None, None], (h, 1, HEAD_DIM))
    col0 = C_OFF // RET_WIDTH
    whole3 = lambda bi, n: (0, 0, 0)
    row = lambda bi, n: bi * nc + n
    return pl.pallas_call(
        _ret_kernel,
        grid=(b, nc),
        in_specs=[pl.BlockSpec((c, RET_WIDTH), lambda bi, n: (row(bi, n), col0)),
                  pl.BlockSpec((c, RET_WIDTH), lambda bi, n: (row(bi, n), col0 + 1)),
                  pl.BlockSpec((c, RET_WIDTH), lambda bi, n: (row(bi, n), col0 + 2)),
                  pl.BlockSpec((c, RET_WIDTH), lambda bi, n: (row(bi, n), col0 + 3)),
                  pl.BlockSpec((c, HEAD_DIM), lambda bi, n: (n, 0)),
                  pl.BlockSpec((c, HEAD_DIM), lambda bi, n: (n, 0)),
                  pl.BlockSpec((h, c, c), whole3),
                  pl.BlockSpec((h, c, HEAD_DIM), whole3),
                  pl.BlockSpec((h, c, HEAD_DIM), whole3),
                  pl.BlockSpec((h, 1, HEAD_DIM), whole3)],
        out_specs=pl.BlockSpec((c, RET_WIDTH), lambda bi, n: (row(bi, n), 0)),
        out_shape=jax.ShapeDtypeStruct((b * s, RET_WIDTH), BF16),
        scratch_shapes=[pltpu.VMEM((h, HEAD_DIM, HEAD_DIM), F32)],
        compiler_params=_cparams("parallel", "arbitrary"),
        name="retention",
    )(proj, proj, proj, proj, cos_t, sin_t, d_mask, xi_b, zeta_b, gch_b)


def _softplus(x):
    return jnp.maximum(x, 0.0) + jnp.log(1.0 + jnp.exp(-jnp.abs(x)))


def _gdn_kernel(alog_ref, dtb_ref, q_ref, k_ref, v_ref, z_ref, cwq_ref, cwk_ref, cwv_ref, ba_ref, bat_ref, nw_ref,
                o_ref, cbuf_ref, state_ref, *, heads):
    hg = pl.program_id(1)
    n = pl.program_id(2)
    rows, chunk, d = GDN_ROWS, GDN_CHUNK, HEAD_DIM
    hw = heads * d

    @pl.when(n == 0)
    def _():
        state_ref[...] = jnp.zeros(state_ref.shape, F32)
        cbuf_ref[0:8, :] = jnp.zeros((8, cbuf_ref.shape[1]), F32)

    for i, (x_ref, cw_ref) in enumerate(((q_ref, cwq_ref), (k_ref, cwk_ref), (v_ref, cwv_ref))):
        cs = slice(i * hw, (i + 1) * hw)
        cbuf_ref[8:8 + rows, cs] = x_ref[...].astype(F32)
        cw = cw_ref[...]
        y = cw[3:4, :] * cbuf_ref[8:8 + rows, cs]
        for tap in range(1, GDN_CONV):
            y = y + cw[3 - tap:4 - tap, :] * cbuf_ref[8 - tap:8 - tap + rows, cs]
        cbuf_ref[0:8, cs] = cbuf_ref[rows:rows + 8, cs]
        cbuf_ref[8:8 + rows, cs] = _silu(y)

    ri = lax.broadcasted_iota(jnp.int32, (rows, rows), 0)
    ci = lax.broadcasted_iota(jnp.int32, (rows, rows), 1)

    def same(blk):
        sh = blk.bit_length() - 1
        return jnp.right_shift(ri, sh) == jnp.right_shift(ci, sh)

    causal = jnp.logical_and(same(chunk), ci <= ri)
    causal_t = jnp.logical_and(same(chunk), ri <= ci)
    strict = jnp.logical_and(same(chunk), ci < ri)
    lane = lax.broadcasted_iota(jnp.int32, (rows, BA_WIDTH), 1)
    ba = ba_ref[...]

    hr = range(heads)
    each = lambda f, *ls: [f(*a) for a in zip(*ls)]
    q = [cbuf_ref[8:8 + rows, hh * d:(hh + 1) * d] for hh in hr]
    k = [cbuf_ref[8:8 + rows, hw + hh * d:hw + (hh + 1) * d] for hh in hr]
    v = [cbuf_ref[8:8 + rows, 2 * hw + hh * d:2 * hw + (hh + 1) * d] for hh in hr]
    q = each(lambda t: t * lax.rsqrt(jnp.sum(t * t, axis=-1, keepdims=True) + NORM_EPS) * (d ** -0.5), q)
    k = each(lambda t: t * lax.rsqrt(jnp.sum(t * t, axis=-1, keepdims=True) + NORM_EPS), k)
    kb = each(lambda t: t.astype(BF16), k)

    head = [hg * heads + hh for hh in hr]
    a_neg = [-jnp.exp(alog_ref[h]) for h in head]
    dtb = [dtb_ref[h] for h in head]
    beta_c = [jax.nn.sigmoid(jnp.sum(jnp.where(lane == h, ba, 0.0), axis=1, keepdims=True)) for h in head]
    alpha_c = [jnp.sum(jnp.where(lane == GDN_HEADS + h, ba, 0.0), axis=1, keepdims=True) for h in head]
    alpha_r = [bat_ref[pl.ds(GDN_HEADS + h, 1), :] for h in head]
    g_c = each(lambda a, x, t: a * _softplus(x + t), a_neg, alpha_c, dtb)
    g_r = each(lambda a, x, t: a * _softplus(x + t), a_neg, alpha_r, dtb)
    gc_c = each(lambda g: jnp.sum(jnp.where(causal, g, 0.0), axis=1, keepdims=True), g_r)
    gc_r = each(lambda g: jnp.sum(jnp.where(causal_t, g, 0.0), axis=0, keepdims=True), g_c)
    decay = each(lambda c, r: jnp.exp(jnp.where(causal, c - r, MASKED)), gc_c, gc_r)

    low = each(lambda bc, t, dc: jnp.where(strict, bc * _mm_nt(t, t) * dc, 0.0), beta_c, kb, decay)
    dg = each(lambda t: jnp.where(same(16), t, 0.0), low)
    d2 = each(lambda t: _mm(t, t), dg)
    d4 = each(lambda t: _mm(t, t), d2)
    e1 = each(lambda a, b2: b2 - a - _mm(a, b2), dg, d2)
    d8 = each(lambda t: _mm(t, t), d4)
    e2 = each(lambda a, b2: a + b2 + _mm(a, b2), d4, d8)
    e = each(lambda a, b2: a + b2 + _mm(a, b2), e1, e2)
    for blk in (16, 32):
        off_mask = jnp.logical_and(same(2 * blk), jnp.logical_not(same(blk)))
        off = each(lambda t: jnp.where(off_mask, t, 0.0), low)
        y = each(lambda o_, e_: o_ + _mm(o_, e_), off, e)
        e = each(lambda e_, y_: e_ - y_ - _mm(e_, y_), e, y)

    rhs = each(lambda v_, k_, bc, gc: jnp.concatenate([v_ * bc, k_ * (bc * jnp.exp(gc))], axis=1), v, k, beta_c, gc_c)
    sol = each(lambda r, e_: r + _mm(e_, r), rhs, e)
    qk = each(lambda q_, t, dc: _mm_nt(q_, t) * dc, q, kb, decay)
    qg = each(lambda q_, gc: q_ * jnp.exp(gc), q, gc_c)

    state = [state_ref[hh] for hh in hr]
    v_new = [[] for _ in hr]
    cross = [[] for _ in hr]
    for c in range(rows // chunk):
        r0 = c * chunk
        rs = slice(r0, r0 + chunk)
        both = each(lambda s_, g_, st: _mm(jnp.concatenate([s_[rs, d:], g_[rs]], axis=0), st), sol, qg, state)
        vn = each(lambda s_, bt: s_[rs, :d] - bt[:chunk], sol, both)
        g_last = [gc[r0 + chunk - 1:r0 + chunk, :] for gc in gc_c]
        kd_t = each(lambda k_, gl, gc: jnp.transpose(k_[rs] * jnp.exp(gl - gc[rs])), k, g_last, gc_c)
        state = each(lambda st, gl, kt, vn_: st * jnp.exp(gl) + _mm(kt, vn_), state, g_last, kd_t, vn)
        for hh in hr:
            v_new[hh].append(vn[hh])
            cross[hh].append(both[hh][chunk:])
    for hh in hr:
        hs = slice(hh * d, (hh + 1) * d)
        state_ref[hh] = state[hh]
        o = jnp.concatenate(cross[hh], axis=0) + _mm(qk[hh], jnp.concatenate(v_new[hh], axis=0))
        o = o * lax.rsqrt(jnp.mean(o * o, axis=-1, keepdims=True) + NORM_EPS) * nw_ref[...]
        o_ref[:, hs] = (o * _silu(z_ref[:, hs].astype(F32))).astype(o_ref.dtype)


def _gdn(proj, ba, conv_w, a_log, dt_bias, norm_w, b, s):
    rows, heads = GDN_ROWS, GDN_HEADS_PER_STEP
    nt = s // rows
    hw = heads * HEAD_DIM
    groups = GDN_HEADS // heads
    col0 = B_OFF // hw
    gcol = GDN_WIDTH // hw
    row = lambda bi, hg, n: bi * nt + n
    smem = pl.BlockSpec(memory_space=pltpu.SMEM)

    def xspec(i):
        return pl.BlockSpec((rows, hw), lambda bi, hg, n: (row(bi, hg, n), col0 + i * gcol + hg))

    def cwspec(i):
        return pl.BlockSpec((GDN_CONV, hw), lambda bi, hg, n: (0, i * gcol + hg))

    return pl.pallas_call(
        functools.partial(_gdn_kernel, heads=heads),
        grid=(b, groups, nt),
        in_specs=[smem, smem, xspec(0), xspec(1), xspec(2), xspec(3), cwspec(0), cwspec(1), cwspec(2),
                  pl.BlockSpec((rows, BA_WIDTH), lambda bi, hg, n: (row(bi, hg, n), 0)),
                  pl.BlockSpec((2 * GDN_HEADS, rows), lambda bi, hg, n: (0, row(bi, hg, n))),
                  pl.BlockSpec((1, HEAD_DIM), lambda bi, hg, n: (0, 0))],
        out_specs=pl.BlockSpec((rows, hw), lambda bi, hg, n: (row(bi, hg, n), hg)),
        out_shape=jax.ShapeDtypeStruct((b * s, GDN_WIDTH), BF16),
        scratch_shapes=[pltpu.VMEM((rows + 8, 3 * hw), F32),
                        pltpu.VMEM((heads, HEAD_DIM, HEAD_DIM), F32)],
        compiler_params=_cparams("parallel", "parallel", "arbitrary"),
        name="gdn",
    )(a_log, dt_bias, proj, proj, proj, proj, conv_w, conv_w, conv_w, ba,
      jnp.transpose(ba[:, :2 * GDN_HEADS]), norm_w.reshape(1, HEAD_DIM))


def kernel(x, ln_mix_w, ln_ffn_w, w_in, gdn_conv_w, gdn_a_log, gdn_dt_bias, gdn_norm_w, attn_sinks,
           w_branch, w_out, w_up, ffn_conv_w, ffn_conv_b, w_down, ln_final_w):
    b, s, d = x.shape
    t = b * s
    depth = w_in.shape[0]
    tm = min(1024, s)
    xf = x.reshape(t, d)
    cos_t, sin_t = _rope_tables(s)
    for layer in range(depth):
        wi = w_in[layer]
        ba_lo = B_OFF + 4 * GDN_WIDTH
        w_in_p = _cast_pack_in(wi, min(2048, d))
        w_ba = jnp.pad(wi[:, ba_lo:ba_lo + 2 * GDN_HEADS], ((0, 0), (0, BA_WIDTH - 2 * GDN_HEADS))).astype(BF16)
        w_up_p = _cast_interleave_up(w_up[layer])
        conv_w_p = _interleave_gate_up(ffn_conv_w[layer])
        conv_b_p = _interleave_gate_up(ffn_conv_b[layer]).reshape(1, -1)

        h = _rmsnorm(xf, ln_mix_w[layer], BF16)
        proj = _matmul(h, w_in_p, BF16, tm, 512, "in_proj")
        ba = _matmul(h, w_ba, F32, tm, BA_WIDTH, "in_proj_gates")
        out_a = _swa(proj, attn_sinks[layer], cos_t, sin_t, b, s)
        out_b = _gdn(proj, ba, gdn_conv_w[layer], gdn_a_log[layer], gdn_dt_bias[layer], gdn_norm_w[layer], b, s)
        out_c = _retention(proj, cos_t, sin_t, b, s)
        merged = _branch_merge(out_a, out_b, out_c, _cast_bf16(w_branch[layer], 256), proj, d, tm, 512)
        xf = _matmul_residual(merged, _cast_bf16(w_out[layer], 256), xf, tm, 512, 1, "out_proj")

        h = _rmsnorm(xf, ln_ffn_w[layer], BF16)
        act = _up_glu(h, w_up_p, conv_w_p, conv_b_p, s, tm)
        xf = _matmul_residual(act, _cast_bf16(w_down[layer], 256), xf, tm, 512, 2, "down_proj")
    out = _rmsnorm(xf, ln_final_w, x.dtype)
    return out.reshape(b, s, d)
```

```python
import functools

import jax
import jax.numpy as jnp
from jax import lax
from jax.experimental import pallas as pl
from jax.experimental.pallas import tpu as pltpu

HEAD_DIM = 128
NORM_EPS = 1e-6
ROPE_THETA = 10000.0
SWA_Q_HEADS = 16
SWA_KV_HEADS = 4
SWA_GROUP = SWA_Q_HEADS // SWA_KV_HEADS
SWA_BLOCK = 128
GDN_HEADS = 8
GDN_CONV = 4
GDN_CHUNK = 64
GDN_ROWS = 256
GDN_HEADS_PER_STEP = 4
RET_HEADS = 8
RET_CHUNK = 128
FFN_CONV = 3
N_BRANCHES = 3

SWA_WIDTH = SWA_Q_HEADS * HEAD_DIM
SWA_KV_WIDTH = SWA_KV_HEADS * HEAD_DIM
GDN_WIDTH = GDN_HEADS * HEAD_DIM
RET_WIDTH = RET_HEADS * HEAD_DIM

A_OFF = 0
B_OFF = SWA_WIDTH + 2 * SWA_KV_WIDTH
C_OFF = B_OFF + 4 * GDN_WIDTH
G_OFF = C_OFF + 4 * RET_WIDTH
BA_WIDTH = 128

VMEM_LIMIT_BYTES = 52 * 1024 * 1024
FFN_TILE = 256
ROW_TILE = 1024
MASKED = -1e30

F32 = jnp.float32
BF16 = jnp.bfloat16
NT_DIMS = (((1,), (1,)), ((), ()))


def _cparams(*sem):
    return pltpu.CompilerParams(dimension_semantics=sem, vmem_limit_bytes=VMEM_LIMIT_BYTES)


def _mm(a, b):
    return jnp.dot(a.astype(BF16), b.astype(BF16), preferred_element_type=F32)


def _mm_nt(a, b):
    return lax.dot_general(a.astype(BF16), b.astype(BF16), NT_DIMS, preferred_element_type=F32)


def _silu(x):
    return x * jax.nn.sigmoid(x)


def _rope(x, cos, sin_signed):
    return x * cos + pltpu.roll(x, HEAD_DIM // 2, 1) * sin_signed


def _rmsnorm_kernel(x_ref, w_ref, o_ref):
    x = x_ref[...]
    y = x * lax.rsqrt(jnp.mean(x * x, axis=-1, keepdims=True) + NORM_EPS)
    o_ref[...] = (y * w_ref[...]).astype(o_ref.dtype)


def _rmsnorm(x, w, out_dtype, tm=256):
    t, d = x.shape
    return pl.pallas_call(
        _rmsnorm_kernel,
        grid=(t // tm,),
        in_specs=[pl.BlockSpec((tm, d), lambda i: (i, 0)),
                  pl.BlockSpec((1, d), lambda i: (0, 0))],
        out_specs=pl.BlockSpec((tm, d), lambda i: (i, 0)),
        out_shape=jax.ShapeDtypeStruct((t, d), out_dtype),
        compiler_params=_cparams("parallel"),
        name="rmsnorm",
    )(x, w.reshape(1, d))


def _mm_kernel(x_ref, w_ref, o_ref):
    o_ref[...] = jnp.dot(x_ref[...], w_ref[...],
                         preferred_element_type=F32).astype(o_ref.dtype)


def _matmul(x, w, out_dtype, tm, tn, name):
    m, k = x.shape
    n = w.shape[1]
    return pl.pallas_call(
        _mm_kernel,
        grid=(m // tm, n // tn),
        in_specs=[pl.BlockSpec((tm, k), lambda i, j: (i, 0)),
                  pl.BlockSpec((k, tn), lambda i, j: (0, j))],
        out_specs=pl.BlockSpec((tm, tn), lambda i, j: (i, j)),
        out_shape=jax.ShapeDtypeStruct((m, n), out_dtype),
        compiler_params=_cparams("parallel", "arbitrary"),
        name=name,
    )(x, w)


def _mm_res_kernel(x_ref, w_ref, r_ref, o_ref, acc_ref, *, nk):
    kk = pl.program_id(2)
    part = jnp.dot(x_ref[...], w_ref[...], preferred_element_type=F32)
    if nk == 1:
        o_ref[...] = r_ref[...] + part
    else:
        @pl.when(kk == 0)
        def _():
            acc_ref[...] = part

        @pl.when(jnp.logical_and(kk > 0, kk < nk - 1))
        def _():
            acc_ref[...] += part

        @pl.when(kk == nk - 1)
        def _():
            o_ref[...] = r_ref[...] + (acc_ref[...] + part)


def _matmul_residual(x, w, res, tm, tn, nk, name):
    m, k = x.shape
    n = w.shape[1]
    tk = k // nk
    return pl.pallas_call(
        functools.partial(_mm_res_kernel, nk=nk),
        grid=(m // tm, n // tn, nk),
        in_specs=[pl.BlockSpec((tm, tk), lambda i, j, kk: (i, kk)),
                  pl.BlockSpec((tk, tn), lambda i, j, kk: (kk, j)),
                  pl.BlockSpec((tm, tn), lambda i, j, kk: (i, j))],
        out_specs=pl.BlockSpec((tm, tn), lambda i, j, kk: (i, j)),
        out_shape=jax.ShapeDtypeStruct((m, n), F32),
        scratch_shapes=[pltpu.VMEM((tm, tn), F32)],
        compiler_params=_cparams("parallel", "arbitrary", "arbitrary"),
        name=name,
    )(x, w, res)


def _merge_kernel(a_ref, b_ref, c_ref, wa_ref, wb_ref, wc_ref, ga_ref, gb_ref, gc_ref, o_ref):
    def branch(x_ref, w_ref, g_ref):
        p = jnp.dot(x_ref[...], w_ref[...], preferred_element_type=F32)
        return jax.nn.sigmoid(g_ref[...].astype(F32)) * p

    merged = branch(a_ref, wa_ref, ga_ref) + branch(b_ref, wb_ref, gb_ref) + branch(c_ref, wc_ref, gc_ref)
    o_ref[...] = merged.astype(o_ref.dtype)


def _branch_merge(out_a, out_b, out_c, w_branch, proj, d_model, tm, tn):
    t = out_a.shape[0]
    gate_blk = G_OFF // tn
    d_blk = d_model // tn
    b_row = SWA_WIDTH // GDN_WIDTH
    return pl.pallas_call(
        _merge_kernel,
        grid=(t // tm, d_model // tn),
        in_specs=[pl.BlockSpec((tm, SWA_WIDTH), lambda i, j: (i, 0)),
                  pl.BlockSpec((tm, GDN_WIDTH), lambda i, j: (i, 0)),
                  pl.BlockSpec((tm, RET_WIDTH), lambda i, j: (i, 0)),
                  pl.BlockSpec((SWA_WIDTH, tn), lambda i, j: (0, j)),
                  pl.BlockSpec((GDN_WIDTH, tn), lambda i, j: (b_row, j)),
                  pl.BlockSpec((RET_WIDTH, tn), lambda i, j: (b_row + 1, j)),
                  pl.BlockSpec((tm, tn), lambda i, j: (i, gate_blk + j)),
                  pl.BlockSpec((tm, tn), lambda i, j: (i, gate_blk + d_blk + j)),
                  pl.BlockSpec((tm, tn), lambda i, j: (i, gate_blk + 2 * d_blk + j))],
        out_specs=pl.BlockSpec((tm, tn), lambda i, j: (i, j)),
        out_shape=jax.ShapeDtypeStruct((t, d_model), BF16),
        compiler_params=_cparams("parallel", "arbitrary"),
        name="branch_merge",
    )(out_a, out_b, out_c, w_branch, w_branch, w_branch, proj, proj, proj)


def _upglu_kernel(x_ref, w_ref, cw_ref, cb_ref, o_ref, u_ref, *, tm, tiles_per_seq):
    mi = pl.program_id(1)

    @pl.when(mi % tiles_per_seq == 0)
    def _():
        u_ref[0:8, :] = jnp.zeros((8, u_ref.shape[1]), F32)

    u_ref[8:8 + tm, :] = jnp.dot(x_ref[...], w_ref[...], preferred_element_type=F32)
    cw = cw_ref[...]
    y = cb_ref[...] + cw[2:3, :] * u_ref[8:8 + tm, :]
    y = y + cw[1:2, :] * u_ref[7:7 + tm, :]
    y = y + cw[0:1, :] * u_ref[6:6 + tm, :]
    u_ref[0:8, :] = u_ref[tm:tm + 8, :]
    o_ref[...] = (_silu(y[:, :FFN_TILE]) * y[:, FFN_TILE:]).astype(o_ref.dtype)


def _up_glu(h, w_up_p, conv_w_p, conv_b_p, seq, tm):
    t, d = h.shape
    n2 = w_up_p.shape[1]
    nblk = n2 // (2 * FFN_TILE)
    return pl.pallas_call(
        functools.partial(_upglu_kernel, tm=tm, tiles_per_seq=seq // tm),
        grid=(nblk, t // tm),
        in_specs=[pl.BlockSpec((tm, d), lambda j, i: (i, 0)),
                  pl.BlockSpec((d, 2 * FFN_TILE), lambda j, i: (0, j)),
                  pl.BlockSpec((FFN_CONV, 2 * FFN_TILE), lambda j, i: (0, j)),
                  pl.BlockSpec((1, 2 * FFN_TILE), lambda j, i: (0, j))],
        out_specs=pl.BlockSpec((tm, FFN_TILE), lambda j, i: (i, j)),
        out_shape=jax.ShapeDtypeStruct((t, nblk * FFN_TILE), BF16),
        scratch_shapes=[pltpu.VMEM((tm + 8, 2 * FFN_TILE), F32)],
        compiler_params=_cparams("parallel", "arbitrary"),
        name="up_conv_glu",
    )(h, w_up_p, conv_w_p, conv_b_p)


def _cast_kernel(w_ref, o_ref):
    o_ref[...] = w_ref[...].astype(o_ref.dtype)


def _cast_bf16(w, layer, tk):
    _, k, n = w.shape
    return pl.pallas_call(
        _cast_kernel,
        grid=(k // tk,),
        in_specs=[pl.BlockSpec((None, tk, n), lambda i: (layer, i, 0))],
        out_specs=pl.BlockSpec((tk, n), lambda i: (i, 0)),
        out_shape=jax.ShapeDtypeStruct((k, n), BF16),
        compiler_params=_cparams("parallel"),
        name="cast_bf16",
    )(w)


def _cast_up_kernel(g_ref, u_ref, o_ref):
    o_ref[:, :FFN_TILE] = g_ref[...].astype(o_ref.dtype)
    o_ref[:, FFN_TILE:] = u_ref[...].astype(o_ref.dtype)


def _cast_interleave_up(w_up, layer):
    _, k, n2 = w_up.shape
    nblk = n2 // (2 * FFN_TILE)
    return pl.pallas_call(
        _cast_up_kernel,
        grid=(nblk,),
        in_specs=[pl.BlockSpec((None, k, FFN_TILE), lambda j: (layer, 0, j)),
                  pl.BlockSpec((None, k, FFN_TILE), lambda j: (layer, 0, nblk + j))],
        out_specs=pl.BlockSpec((k, 2 * FFN_TILE), lambda j: (0, j)),
        out_shape=jax.ShapeDtypeStruct((k, n2), BF16),
        compiler_params=_cparams("parallel"),
        name="cast_up",
    )(w_up, w_up)


IN_PACK_TILE = 512
IN_PACK_SHIFT = 2 * GDN_HEADS


def _cast_in_kernel(a_ref, b_ref, o_ref, *, first_shifted):
    j = pl.program_id(1)

    @pl.when(j < first_shifted)
    def _():
        o_ref[...] = a_ref[...].astype(o_ref.dtype)

    @pl.when(j >= first_shifted)
    def _():
        shifted = jnp.concatenate([a_ref[:, IN_PACK_SHIFT:], b_ref[:, :IN_PACK_SHIFT]], axis=1)
        o_ref[...] = shifted.astype(o_ref.dtype)


def _cast_pack_in(w_in, layer, tk):
    _, k, n = w_in.shape
    ba_lo = B_OFF + 4 * GDN_WIDTH
    n_packed = n - IN_PACK_SHIFT
    tn = IN_PACK_TILE
    lanes = HEAD_DIM
    return pl.pallas_call(
        functools.partial(_cast_in_kernel, first_shifted=ba_lo // tn),
        grid=(k // tk, n_packed // tn),
        in_specs=[pl.BlockSpec((None, tk, tn), lambda i, j: (layer, i, j)),
                  pl.BlockSpec((None, tk, lanes), lambda i, j: (layer, i, (tn // lanes) * (j + 1)))],
        out_specs=pl.BlockSpec((tk, tn), lambda i, j: (i, j)),
        out_shape=jax.ShapeDtypeStruct((k, n_packed), BF16),
        compiler_params=_cparams("parallel", "parallel"),
        name="cast_in",
    )(w_in, w_in)


def _interleave_gate_up(a):
    lead = a.shape[:-1]
    f = a.shape[-1] // 2
    a = a.reshape(lead + (2, f // FFN_TILE, FFN_TILE))
    a = jnp.swapaxes(a, -3, -2)
    return a.reshape(lead + (2 * f,))


def _rope_tables(s):
    half = HEAD_DIM // 2
    inv_freq = ROPE_THETA ** (-jnp.arange(half, dtype=F32) / half)
    ang = jnp.arange(s, dtype=F32)[:, None] * inv_freq[None, :]
    cos, sin = jnp.cos(ang), jnp.sin(ang)
    return jnp.concatenate([cos, cos], axis=1), jnp.concatenate([-sin, sin], axis=1)


def _swa_kernel(sink_ref, q_ref, kc_ref, kp_ref, vc_ref, vp_ref, cos_ref, sin_ref, cosp_ref, sinp_ref, o_ref):
    n = pl.program_id(1)
    blk = SWA_BLOCK
    cos, sin = cos_ref[...], sin_ref[...]
    cosp, sinp = cosp_ref[...], sinp_ref[...]
    qi = lax.broadcasted_iota(jnp.int32, (blk, 2 * blk), 0)
    kj = lax.broadcasted_iota(jnp.int32, (blk, 2 * blk), 1)
    no_prev = jnp.where(n > 0, 0, blk)
    valid = jnp.where(kj < blk, kj - qi - no_prev, qi - kj + blk + 1) > 0
    scale = HEAD_DIM ** -0.5
    for hk in range(SWA_KV_HEADS):
        sl = slice(hk * HEAD_DIM, (hk + 1) * HEAD_DIM)
        kc = _rope(kc_ref[:, sl].astype(F32), cos, sin)
        kp = _rope(kp_ref[:, sl].astype(F32), cosp, sinp)
        k2 = jnp.concatenate([kp, kc], axis=0).astype(BF16)
        v2 = jnp.concatenate([vp_ref[:, sl], vc_ref[:, sl]], axis=0)
        qs = []
        for g in range(SWA_GROUP):
            hq = hk * SWA_GROUP + g
            qh = _rope(q_ref[:, hq * HEAD_DIM:(hq + 1) * HEAD_DIM].astype(F32), cos, sin) * scale
            qs.append(qh.astype(BF16))
        scores = _mm_nt(jnp.concatenate(qs, axis=0), k2)
        probs, inv_den = [], []
        for g in range(SWA_GROUP):
            sink = sink_ref[hk * SWA_GROUP + g]
            sg = jnp.where(valid, scores[g * blk:(g + 1) * blk], MASKED)
            m = jnp.maximum(jnp.max(sg, axis=1, keepdims=True), sink)
            p = jnp.exp(sg - m)
            den = jnp.sum(p, axis=1, keepdims=True) + jnp.exp(sink - m)
            probs.append(p.astype(BF16))
            inv_den.append(1.0 / den)
        out = jnp.dot(jnp.concatenate(probs, axis=0), v2, preferred_element_type=F32)
        for g in range(SWA_GROUP):
            hq = hk * SWA_GROUP + g
            o_ref[:, hq * HEAD_DIM:(hq + 1) * HEAD_DIM] = (out[g * blk:(g + 1) * blk] * inv_den[g]).astype(o_ref.dtype)


def _swa(proj, sinks, cos_t, sin_t, b, s):
    nb = s // SWA_BLOCK
    kcol = (A_OFF + SWA_WIDTH) // SWA_KV_WIDTH
    vcol = kcol + 1
    cur = lambda bi, n: (bi * nb + n, 0)
    prev_rows = lambda bi, n: bi * nb + jnp.maximum(n - 1, 0)
    return pl.pallas_call(
        _swa_kernel,
        grid=(b, nb),
        in_specs=[pl.BlockSpec(memory_space=pltpu.SMEM),
                  pl.BlockSpec((SWA_BLOCK, SWA_WIDTH), cur),
                  pl.BlockSpec((SWA_BLOCK, SWA_KV_WIDTH), lambda bi, n: (bi * nb + n, kcol)),
                  pl.BlockSpec((SWA_BLOCK, SWA_KV_WIDTH), lambda bi, n: (prev_rows(bi, n), kcol)),
                  pl.BlockSpec((SWA_BLOCK, SWA_KV_WIDTH), lambda bi, n: (bi * nb + n, vcol)),
                  pl.BlockSpec((SWA_BLOCK, SWA_KV_WIDTH), lambda bi, n: (prev_rows(bi, n), vcol)),
                  pl.BlockSpec((SWA_BLOCK, HEAD_DIM), lambda bi, n: (n, 0)),
                  pl.BlockSpec((SWA_BLOCK, HEAD_DIM), lambda bi, n: (n, 0)),
                  pl.BlockSpec((SWA_BLOCK, HEAD_DIM), lambda bi, n: (jnp.maximum(n - 1, 0), 0)),
                  pl.BlockSpec((SWA_BLOCK, HEAD_DIM), lambda bi, n: (jnp.maximum(n - 1, 0), 0))],
        out_specs=pl.BlockSpec((SWA_BLOCK, SWA_WIDTH), cur),
        out_shape=jax.ShapeDtypeStruct((b * s, SWA_WIDTH), BF16),
        compiler_params=_cparams("parallel", "parallel"),
        name="swa",
    )(sinks, proj, proj, proj, proj, proj, cos_t, sin_t, cos_t, sin_t)


def _ret_kernel(q_ref, k_ref, v_ref, g_ref, cos_ref, sin_ref, dm_ref, xi_ref, zeta_ref, gch_ref, o_ref, state_ref):
    @pl.when(pl.program_id(1) == 0)
    def _():
        state_ref[...] = jnp.zeros(state_ref.shape, F32)

    cos, sin = cos_ref[...], sin_ref[...]
    for h in range(RET_HEADS):
        sl = slice(h * HEAD_DIM, (h + 1) * HEAD_DIM)
        q = _rope(q_ref[:, sl].astype(F32), cos, sin)
        k = _rope(k_ref[:, sl].astype(F32), cos, sin) * (HEAD_DIM ** -0.5)
        v = v_ref[:, sl]
        scores = _mm_nt(q, k) * dm_ref[h]
        inner = _mm(scores, v)
        state = state_ref[h]
        cross = _mm(q, state) * xi_ref[h]
        kz_t = jnp.transpose(k * zeta_ref[h])
        state_ref[h] = state * gch_ref[h] + _mm(kz_t, v)
        o = inner + cross
        mu = jnp.mean(o, axis=-1, keepdims=True)
        var = jnp.mean(jnp.square(o - mu), axis=-1, keepdims=True)
        o = (o - mu) * lax.rsqrt(var + NORM_EPS)
        o_ref[:, sl] = (_silu(g_ref[:, sl].astype(F32)) * o).astype(o_ref.dtype)


def _retention(proj, cos_t, sin_t, b, s):
    c = RET_CHUNK
    nc = s // c
    h = RET_HEADS
    log_gamma = jnp.log(1.0 - 2.0 ** (-5.0 - jnp.arange(h, dtype=F32)))
    idx = jnp.arange(c, dtype=F32)
    rel = idx[:, None] - idx[None, :]
    d_mask = jnp.where(rel >= 0, jnp.exp(jnp.maximum(rel, 0.0)[None] * log_gamma[:, None, None]), 0.0)
    xi = jnp.exp((idx + 1.0)[None, :] * log_gamma[:, None])
    zeta = jnp.exp((c - 1.0 - idx)[None, :] * log_gamma[:, None])
    gamma_chunk = jnp.exp(c * log_gamma)
    xi_b = jnp.broadcast_to(xi[:, :, None], (h, c, HEAD_DIM))
    zeta_b = jnp.broadcast_to(zeta[:, :, None], (h, c, HEAD_DIM))
    gch_b = jnp.broadcast_to(gamma_chunk[:, None, None], (h, 1, HEAD_DIM))
    col0 = C_OFF // RET_WIDTH
    whole3 = lambda bi, n: (0, 0, 0)
    row = lambda bi, n: bi * nc + n
    return pl.pallas_call(
        _ret_kernel,
        grid=(b, nc),
        in_specs=[pl.BlockSpec((c, RET_WIDTH), lambda bi, n: (row(bi, n), col0)),
                  pl.BlockSpec((c, RET_WIDTH), lambda bi, n: (row(bi, n), col0 + 1)),
                  pl.BlockSpec((c, RET_WIDTH), lambda bi, n: (row(bi, n), col0 + 2)),
                  pl.BlockSpec((c, RET_WIDTH), lambda bi, n: (row(bi, n), col0 + 3)),
                  pl.BlockSpec((c, HEAD_DIM), lambda bi, n: (n, 0)),
                  pl.BlockSpec((c, HEAD_DIM), lambda bi, n: (n, 0)),
                  pl.BlockSpec((h, c, c), whole3),
                  pl.BlockSpec((h, c, HEAD_DIM), whole3),
                  pl.BlockSpec((h, c, HEAD_DIM), whole3),
                  pl.BlockSpec((h, 1, HEAD_DIM), whole3)],
        out_specs=pl.BlockSpec((c, RET_WIDTH), lambda bi, n: (row(bi, n), 0)),
        out_shape=jax.ShapeDtypeStruct((b * s, RET_WIDTH), BF16),
        scratch_shapes=[pltpu.VMEM((h, HEAD_DIM, HEAD_DIM), F32)],
        compiler_params=_cparams("parallel", "arbitrary"),
        name="retention",
    )(proj, proj, proj, proj, cos_t, sin_t, d_mask, xi_b, zeta_b, gch_b)


def _softplus(x):
    return jnp.maximum(x, 0.0) + jnp.log(1.0 + jnp.exp(-jnp.abs(x)))


def _gdn_kernel(alog_ref, dtb_ref, q_ref, k_ref, v_ref, z_ref, cwq_ref, cwk_ref, cwv_ref, ba_ref, bat_ref, nw_ref,
                o_ref, cbuf_ref, state_ref, *, heads):
    hg = pl.program_id(1)
    n = pl.program_id(2)
    rows, chunk, d = GDN_ROWS, GDN_CHUNK, HEAD_DIM
    hw = heads * d

    @pl.when(n == 0)
    def _():
        state_ref[...] = jnp.zeros(state_ref.shape, F32)
        cbuf_ref[0:8, :] = jnp.zeros((8, cbuf_ref.shape[1]), F32)

    for i, (x_ref, cw_ref) in enumerate(((q_ref, cwq_ref), (k_ref, cwk_ref), (v_ref, cwv_ref))):
        cs = slice(i * hw, (i + 1) * hw)
        cbuf_ref[8:8 + rows, cs] = x_ref[...].astype(F32)
        cw = cw_ref[...]
        y = cw[3:4, :] * cbuf_ref[8:8 + rows, cs]
        for tap in range(1, GDN_CONV):
            y = y + cw[3 - tap:4 - tap, :] * cbuf_ref[8 - tap:8 - tap + rows, cs]
        cbuf_ref[0:8, cs] = cbuf_ref[rows:rows + 8, cs]
        cbuf_ref[8:8 + rows, cs] = _silu(y)

    ri = lax.broadcasted_iota(jnp.int32, (rows, rows), 0)
    ci = lax.broadcasted_iota(jnp.int32, (rows, rows), 1)

    def same(blk):
        sh = blk.bit_length() - 1
        return jnp.right_shift(ri, sh) == jnp.right_shift(ci, sh)

    causal = jnp.logical_and(same(chunk), ci <= ri)
    causal_t = jnp.logical_and(same(chunk), ri <= ci)
    strict = jnp.logical_and(same(chunk), ci < ri)
    lane = lax.broadcasted_iota(jnp.int32, (rows, BA_WIDTH), 1)
    ba = ba_ref[...]

    hr = range(heads)
    each = lambda f, *ls: [f(*a) for a in zip(*ls)]
    q = [cbuf_ref[8:8 + rows, hh * d:(hh + 1) * d] for hh in hr]
    k = [cbuf_ref[8:8 + rows, hw + hh * d:hw + (hh + 1) * d] for hh in hr]
    v = [cbuf_ref[8:8 + rows, 2 * hw + hh * d:2 * hw + (hh + 1) * d] for hh in hr]
    q = each(lambda t: t * lax.rsqrt(jnp.sum(t * t, axis=-1, keepdims=True) + NORM_EPS) * (d ** -0.5), q)
    k = each(lambda t: t * lax.rsqrt(jnp.sum(t * t, axis=-1, keepdims=True) + NORM_EPS), k)
    kb = each(lambda t: t.astype(BF16), k)

    head = [hg * heads + hh for hh in hr]
    a_neg = [-jnp.exp(alog_ref[h]) for h in head]
    dtb = [dtb_ref[h] for h in head]
    beta_c = [jax.nn.sigmoid(jnp.sum(jnp.where(lane == h, ba, 0.0), axis=1, keepdims=True)) for h in head]
    alpha_c = [jnp.sum(jnp.where(lane == GDN_HEADS + h, ba, 0.0), axis=1, keepdims=True) for h in head]
    alpha_r = [bat_ref[pl.ds(GDN_HEADS + h, 1), :] for h in head]
    g_c = each(lambda a, x, t: a * _softplus(x + t), a_neg, alpha_c, dtb)
    g_r = each(lambda a, x, t: a * _softplus(x + t), a_neg, alpha_r, dtb)
    gc_c = each(lambda g: jnp.sum(jnp.where(causal, g, 0.0), axis=1, keepdims=True), g_r)
    gc_r = each(lambda g: jnp.sum(jnp.where(causal_t, g, 0.0), axis=0, keepdims=True), g_c)
    decay = each(lambda c, r: jnp.exp(jnp.where(causal, c - r, MASKED)), gc_c, gc_r)

    low = each(lambda bc, t, dc: jnp.where(strict, bc * _mm_nt(t, t) * dc, 0.0), beta_c, kb, decay)
    dg = each(lambda t: jnp.where(same(16), t, 0.0), low)
    d2 = each(lambda t: _mm(t, t), dg)
    d4 = each(lambda t: _mm(t, t), d2)
    e1 = each(lambda a, b2: b2 - a - _mm(a, b2), dg, d2)
    d8 = each(lambda t: _mm(t, t), d4)
    e2 = each(lambda a, b2: a + b2 + _mm(a, b2), d4, d8)
    e = each(lambda a, b2: a + b2 + _mm(a, b2), e1, e2)
    for blk in (16, 32):
        off_mask = jnp.logical_and(same(2 * blk), jnp.logical_not(same(blk)))
        off = each(lambda t: jnp.where(off_mask, t, 0.0), low)
        y = each(lambda o_, e_: o_ + _mm(o_, e_), off, e)
        e = each(lambda e_, y_: e_ - y_ - _mm(e_, y_), e, y)

    rhs = each(lambda v_, k_, bc, gc: jnp.concatenate([v_ * bc, k_ * (bc * jnp.exp(gc))], axis=1), v, k, beta_c, gc_c)
    sol = each(lambda r, e_: r + _mm(e_, r), rhs, e)
    qk = each(lambda q_, t, dc: _mm_nt(q_, t) * dc, q, kb, decay)
    qg = each(lambda q_, gc: q_ * jnp.exp(gc), q, gc_c)

    state = [state_ref[hh] for hh in hr]
    v_new = [[] for _ in hr]
    cross = [[] for _ in hr]
    for c in range(rows // chunk):
        r0 = c * chunk
        rs = slice(r0, r0 + chunk)
        both = each(lambda s_, g_, st: _mm(jnp.concatenate([s_[rs, d:], g_[rs]], axis=0), st), sol, qg, state)
        vn = each(lambda s_, bt: s_[rs, :d] - bt[:chunk], sol, both)
        g_last = [gc[r0 + chunk - 1:r0 + chunk, :] for gc in gc_c]
        kd_t = each(lambda k_, gl, gc: jnp.transpose(k_[rs] * jnp.exp(gl - gc[rs])), k, g_last, gc_c)
        state = each(lambda st, gl, kt, vn_: st * jnp.exp(gl) + _mm(kt, vn_), state, g_last, kd_t, vn)
        for hh in hr:
            v_new[hh].append(vn[hh])
            cross[hh].append(both[hh][chunk:])
    for hh in hr:
        hs = slice(hh * d, (hh + 1) * d)
        state_ref[hh] = state[hh]
        o = jnp.concatenate(cross[hh], axis=0) + _mm(qk[hh], jnp.concatenate(v_new[hh], axis=0))
        o = o * lax.rsqrt(jnp.mean(o * o, axis=-1, keepdims=True) + NORM_EPS) * nw_ref[...]
        o_ref[:, hs] = (o * _silu(z_ref[:, hs].astype(F32))).astype(o_ref.dtype)


def _gdn(proj, ba, conv_w, a_log, dt_bias, norm_w, b, s):
    rows, heads = GDN_ROWS, GDN_HEADS_PER_STEP
    nt = s // rows
    hw = heads * HEAD_DIM
    groups = GDN_HEADS // heads
    col0 = B_OFF // hw
    gcol = GDN_WIDTH // hw
    row = lambda bi, hg, n: bi * nt + n
    smem = pl.BlockSpec(memory_space=pltpu.SMEM)

    def xspec(i):
        return pl.BlockSpec((rows, hw), lambda bi, hg, n: (row(bi, hg, n), col0 + i * gcol + hg))

    def cwspec(i):
        return pl.BlockSpec((GDN_CONV, hw), lambda bi, hg, n: (0, i * gcol + hg))

    return pl.pallas_call(
        functools.partial(_gdn_kernel, heads=heads),
        grid=(b, groups, nt),
        in_specs=[smem, smem, xspec(0), xspec(1), xspec(2), xspec(3), cwspec(0), cwspec(1), cwspec(2),
                  pl.BlockSpec((rows, BA_WIDTH), lambda bi, hg, n: (row(bi, hg, n), 0)),
                  pl.BlockSpec((2 * GDN_HEADS, rows), lambda bi, hg, n: (0, row(bi, hg, n))),
                  pl.BlockSpec((1, HEAD_DIM), lambda bi, hg, n: (0, 0))],
        out_specs=pl.BlockSpec((rows, hw), lambda bi, hg, n: (row(bi, hg, n), hg)),
        out_shape=jax.ShapeDtypeStruct((b * s, GDN_WIDTH), BF16),
        scratch_shapes=[pltpu.VMEM((rows + 8, 3 * hw), F32),
                        pltpu.VMEM((heads, HEAD_DIM, HEAD_DIM), F32)],
        compiler_params=_cparams("parallel", "parallel", "arbitrary"),
        name="gdn",
    )(a_log, dt_bias, proj, proj, proj, proj, conv_w, conv_w, conv_w, ba,
      jnp.transpose(ba[:, :2 * GDN_HEADS]), norm_w.reshape(1, HEAD_DIM))


def kernel(x, ln_mix_w, ln_ffn_w, w_in, gdn_conv_w, gdn_a_log, gdn_dt_bias, gdn_norm_w, attn_sinks,
           w_branch, w_out, w_up, ffn_conv_w, ffn_conv_b, w_down, ln_final_w):
    b, s, d = x.shape
    t = b * s
    depth = w_in.shape[0]
    tm = min(ROW_TILE, s)
    xf = x.reshape(t, d)
    cos_t, sin_t = _rope_tables(s)
    for layer in range(depth):
        ba_lo = B_OFF + 4 * GDN_WIDTH
        w_in_p = _cast_pack_in(w_in, layer, min(2048, d))
        w_ba = jnp.pad(w_in[layer, :, ba_lo:ba_lo + 2 * GDN_HEADS],
                       ((0, 0), (0, BA_WIDTH - 2 * GDN_HEADS))).astype(BF16)
        w_up_p = _cast_interleave_up(w_up, layer)
        conv_w_p = _interleave_gate_up(ffn_conv_w[layer])
        conv_b_p = _interleave_gate_up(ffn_conv_b[layer]).reshape(1, -1)

        h = _rmsnorm(xf, ln_mix_w[layer], BF16)
        proj = _matmul(h, w_in_p, BF16, tm, 512, "in_proj")
        ba = _matmul(h, w_ba, F32, tm, BA_WIDTH, "in_proj_gates")
        out_a = _swa(proj, attn_sinks[layer], cos_t, sin_t, b, s)
        out_b = _gdn(proj, ba, gdn_conv_w[layer], gdn_a_log[layer], gdn_dt_bias[layer], gdn_norm_w[layer], b, s)
        out_c = _retention(proj, cos_t, sin_t, b, s)
        merged = _branch_merge(out_a, out_b, out_c, _cast_bf16(w_branch, layer, 256), proj, d, tm, 512)
        xf = _matmul_residual(merged, _cast_bf16(w_out, layer, 256), xf, tm, 512, 1, "out_proj")

        h = _rmsnorm(xf, ln_ffn_w[layer], BF16)
        act = _up_glu(h, w_up_p, conv_w_p, conv_b_p, s, tm)
        xf = _matmul_residual(act, _cast_bf16(w_down, layer, 256), xf, tm, 512, 2, "down_proj")
    out = _rmsnorm(xf, ln_final_w, x.dtype)
    return out.reshape(b, s, d)
```

```python
import functools

import jax
import jax.numpy as jnp
from jax import lax
from jax.experimental import pallas as pl
from jax.experimental.pallas import tpu as pltpu

HEAD_DIM = 128
NORM_EPS = 1e-6
ROPE_THETA = 10000.0
SWA_Q_HEADS = 16
SWA_KV_HEADS = 4
SWA_GROUP = SWA_Q_HEADS // SWA_KV_HEADS
SWA_BLOCK = 128
GDN_HEADS = 8
GDN_CONV = 4
GDN_CHUNK = 64
GDN_ROWS = 256
GDN_HEADS_PER_STEP = 4
RET_HEADS = 8
RET_CHUNK = 128
FFN_CONV = 3
N_BRANCHES = 3

SWA_WIDTH = SWA_Q_HEADS * HEAD_DIM
SWA_KV_WIDTH = SWA_KV_HEADS * HEAD_DIM
GDN_WIDTH = GDN_HEADS * HEAD_DIM
RET_WIDTH = RET_HEADS * HEAD_DIM

A_OFF = 0
B_OFF = SWA_WIDTH + 2 * SWA_KV_WIDTH
C_OFF = B_OFF + 4 * GDN_WIDTH
G_OFF = C_OFF + 4 * RET_WIDTH
BA_WIDTH = 128

VMEM_LIMIT_BYTES = 52 * 1024 * 1024
FFN_TILE = 256
ROW_TILE = 1024
MASKED = -1e30

F32 = jnp.float32
BF16 = jnp.bfloat16
NT_DIMS = (((1,), (1,)), ((), ()))


def _cparams(*sem):
    return pltpu.CompilerParams(dimension_semantics=sem, vmem_limit_bytes=VMEM_LIMIT_BYTES)


def _mm(a, b):
    return jnp.dot(a.astype(BF16), b.astype(BF16), preferred_element_type=F32)


def _mm_nt(a, b):
    return lax.dot_general(a.astype(BF16), b.astype(BF16), NT_DIMS, preferred_element_type=F32)


def _silu(x):
    return x * jax.nn.sigmoid(x)


def _rope(x, cos, sin_signed):
    return x * cos + pltpu.roll(x, HEAD_DIM // 2, 1) * sin_signed


def _rmsnorm_kernel(x_ref, w_ref, o_ref):
    x = x_ref[...]
    y = x * lax.rsqrt(jnp.mean(x * x, axis=-1, keepdims=True) + NORM_EPS)
    o_ref[...] = (y * w_ref[...]).astype(o_ref.dtype)


def _rmsnorm(x, w, out_dtype, tm=256):
    t, d = x.shape
    return pl.pallas_call(
        _rmsnorm_kernel,
        grid=(t // tm,),
        in_specs=[pl.BlockSpec((tm, d), lambda i: (i, 0)),
                  pl.BlockSpec((1, d), lambda i: (0, 0))],
        out_specs=pl.BlockSpec((tm, d), lambda i: (i, 0)),
        out_shape=jax.ShapeDtypeStruct((t, d), out_dtype),
        compiler_params=_cparams("parallel"),
        name="rmsnorm",
    )(x, w.reshape(1, d))


def _mm_kernel(x_ref, w_ref, o_ref):
    o_ref[...] = jnp.dot(x_ref[...], w_ref[...],
                         preferred_element_type=F32).astype(o_ref.dtype)


def _matmul(x, w, out_dtype, tm, tn, name):
    m, k = x.shape
    n = w.shape[1]
    return pl.pallas_call(
        _mm_kernel,
        grid=(m // tm, n // tn),
        in_specs=[pl.BlockSpec((tm, k), lambda i, j: (i, 0)),
                  pl.BlockSpec((k, tn), lambda i, j: (0, j))],
        out_specs=pl.BlockSpec((tm, tn), lambda i, j: (i, j)),
        out_shape=jax.ShapeDtypeStruct((m, n), out_dtype),
        compiler_params=_cparams("parallel", "arbitrary"),
        name=name,
    )(x, w)


def _gate_proj_kernel(x_ref, w_ref, o_ref):
    o_ref[...] = jnp.dot(x_ref[...], w_ref[...].astype(BF16), preferred_element_type=F32)


def _gate_proj(x, w_in, layer, tm):
    m, k = x.shape
    col_blk = (B_OFF + 4 * GDN_WIDTH) // BA_WIDTH
    return pl.pallas_call(
        _gate_proj_kernel,
        grid=(m // tm,),
        in_specs=[pl.BlockSpec((tm, k), lambda i: (i, 0)),
                  pl.BlockSpec((None, k, BA_WIDTH), lambda i: (layer, 0, col_blk))],
        out_specs=pl.BlockSpec((tm, BA_WIDTH), lambda i: (i, 0)),
        out_shape=jax.ShapeDtypeStruct((m, BA_WIDTH), F32),
        compiler_params=_cparams("parallel"),
        name="in_proj_gates",
    )(x, w_in)


def _mm_res_kernel(x_ref, w_ref, r_ref, o_ref, acc_ref, *, nk):
    kk = pl.program_id(2)
    part = jnp.dot(x_ref[...], w_ref[...], preferred_element_type=F32)
    if nk == 1:
        o_ref[...] = r_ref[...] + part
    else:
        @pl.when(kk == 0)
        def _():
            acc_ref[...] = part

        @pl.when(jnp.logical_and(kk > 0, kk < nk - 1))
        def _():
            acc_ref[...] += part

        @pl.when(kk == nk - 1)
        def _():
            o_ref[...] = r_ref[...] + (acc_ref[...] + part)


def _matmul_residual(x, w, res, tm, tn, nk, name):
    m, k = x.shape
    n = w.shape[1]
    tk = k // nk
    return pl.pallas_call(
        functools.partial(_mm_res_kernel, nk=nk),
        grid=(m // tm, n // tn, nk),
        in_specs=[pl.BlockSpec((tm, tk), lambda i, j, kk: (i, kk)),
                  pl.BlockSpec((tk, tn), lambda i, j, kk: (kk, j)),
                  pl.BlockSpec((tm, tn), lambda i, j, kk: (i, j))],
        out_specs=pl.BlockSpec((tm, tn), lambda i, j, kk: (i, j)),
        out_shape=jax.ShapeDtypeStruct((m, n), F32),
        scratch_shapes=[pltpu.VMEM((tm, tn), F32)],
        compiler_params=_cparams("parallel", "arbitrary", "arbitrary"),
        name=name,
    )(x, w, res)


def _merge_kernel(a_ref, b_ref, c_ref, wa_ref, wb_ref, wc_ref, ga_ref, gb_ref, gc_ref, o_ref):
    def branch(x_ref, w_ref, g_ref):
        p = jnp.dot(x_ref[...], w_ref[...], preferred_element_type=F32)
        return jax.nn.sigmoid(g_ref[...].astype(F32)) * p

    merged = branch(a_ref, wa_ref, ga_ref) + branch(b_ref, wb_ref, gb_ref) + branch(c_ref, wc_ref, gc_ref)
    o_ref[...] = merged.astype(o_ref.dtype)


def _branch_merge(out_a, out_b, out_c, w_branch, proj, d_model, tm, tn):
    t = out_a.shape[0]
    gate_blk = G_OFF // tn
    d_blk = d_model // tn
    b_row = SWA_WIDTH // GDN_WIDTH
    return pl.pallas_call(
        _merge_kernel,
        grid=(t // tm, d_model // tn),
        in_specs=[pl.BlockSpec((tm, SWA_WIDTH), lambda i, j: (i, 0)),
                  pl.BlockSpec((tm, GDN_WIDTH), lambda i, j: (i, 0)),
                  pl.BlockSpec((tm, RET_WIDTH), lambda i, j: (i, 0)),
                  pl.BlockSpec((SWA_WIDTH, tn), lambda i, j: (0, j)),
                  pl.BlockSpec((GDN_WIDTH, tn), lambda i, j: (b_row, j)),
                  pl.BlockSpec((RET_WIDTH, tn), lambda i, j: (b_row + 1, j)),
                  pl.BlockSpec((tm, tn), lambda i, j: (i, gate_blk + j)),
                  pl.BlockSpec((tm, tn), lambda i, j: (i, gate_blk + d_blk + j)),
                  pl.BlockSpec((tm, tn), lambda i, j: (i, gate_blk + 2 * d_blk + j))],
        out_specs=pl.BlockSpec((tm, tn), lambda i, j: (i, j)),
        out_shape=jax.ShapeDtypeStruct((t, d_model), BF16),
        compiler_params=_cparams("parallel", "arbitrary"),
        name="branch_merge",
    )(out_a, out_b, out_c, w_branch, w_branch, w_branch, proj, proj, proj)


def _upglu_kernel(x_ref, w_ref, cw_ref, cb_ref, o_ref, u_ref, *, tm, tiles_per_seq):
    mi = pl.program_id(1)

    @pl.when(mi % tiles_per_seq == 0)
    def _():
        u_ref[0:8, :] = jnp.zeros((8, u_ref.shape[1]), F32)

    u_ref[8:8 + tm, :] = jnp.dot(x_ref[...], w_ref[...], preferred_element_type=F32)
    cw = cw_ref[...]
    y = cb_ref[...] + cw[2:3, :] * u_ref[8:8 + tm, :]
    y = y + cw[1:2, :] * u_ref[7:7 + tm, :]
    y = y + cw[0:1, :] * u_ref[6:6 + tm, :]
    u_ref[0:8, :] = u_ref[tm:tm + 8, :]
    o_ref[...] = (_silu(y[:, :FFN_TILE]) * y[:, FFN_TILE:]).astype(o_ref.dtype)


def _up_glu(h, w_up_p, conv_w_p, conv_b_p, seq, tm):
    t, d = h.shape
    n2 = w_up_p.shape[1]
    nblk = n2 // (2 * FFN_TILE)
    return pl.pallas_call(
        functools.partial(_upglu_kernel, tm=tm, tiles_per_seq=seq // tm),
        grid=(nblk, t // tm),
        in_specs=[pl.BlockSpec((tm, d), lambda j, i: (i, 0)),
                  pl.BlockSpec((d, 2 * FFN_TILE), lambda j, i: (0, j)),
                  pl.BlockSpec((FFN_CONV, 2 * FFN_TILE), lambda j, i: (0, j)),
                  pl.BlockSpec((1, 2 * FFN_TILE), lambda j, i: (0, j))],
        out_specs=pl.BlockSpec((tm, FFN_TILE), lambda j, i: (i, j)),
        out_shape=jax.ShapeDtypeStruct((t, nblk * FFN_TILE), BF16),
        scratch_shapes=[pltpu.VMEM((tm + 8, 2 * FFN_TILE), F32)],
        compiler_params=_cparams("parallel", "arbitrary"),
        name="up_conv_glu",
    )(h, w_up_p, conv_w_p, conv_b_p)


def _cast_kernel(w_ref, o_ref):
    o_ref[...] = w_ref[...].astype(o_ref.dtype)


def _cast_bf16(w, layer, tk):
    _, k, n = w.shape
    return pl.pallas_call(
        _cast_kernel,
        grid=(k // tk,),
        in_specs=[pl.BlockSpec((None, tk, n), lambda i: (layer, i, 0))],
        out_specs=pl.BlockSpec((tk, n), lambda i: (i, 0)),
        out_shape=jax.ShapeDtypeStruct((k, n), BF16),
        compiler_params=_cparams("parallel"),
        name="cast_bf16",
    )(w)


def _cast_up_kernel(g_ref, u_ref, o_ref):
    o_ref[:, :FFN_TILE] = g_ref[...].astype(o_ref.dtype)
    o_ref[:, FFN_TILE:] = u_ref[...].astype(o_ref.dtype)


def _cast_interleave_up(w_up, layer):
    _, k, n2 = w_up.shape
    nblk = n2 // (2 * FFN_TILE)
    return pl.pallas_call(
        _cast_up_kernel,
        grid=(nblk,),
        in_specs=[pl.BlockSpec((None, k, FFN_TILE), lambda j: (layer, 0, j)),
                  pl.BlockSpec((None, k, FFN_TILE), lambda j: (layer, 0, nblk + j))],
        out_specs=pl.BlockSpec((k, 2 * FFN_TILE), lambda j: (0, j)),
        out_shape=jax.ShapeDtypeStruct((k, n2), BF16),
        compiler_params=_cparams("parallel"),
        name="cast_up",
    )(w_up, w_up)


IN_PACK_TILE = 512
IN_PACK_SHIFT = 2 * GDN_HEADS


def _cast_in_kernel(a_ref, b_ref, o_ref, *, first_shifted):
    j = pl.program_id(1)

    @pl.when(j < first_shifted)
    def _():
        o_ref[...] = a_ref[...].astype(o_ref.dtype)

    @pl.when(j >= first_shifted)
    def _():
        shifted = jnp.concatenate([a_ref[:, IN_PACK_SHIFT:], b_ref[:, :IN_PACK_SHIFT]], axis=1)
        o_ref[...] = shifted.astype(o_ref.dtype)


def _cast_pack_in(w_in, layer, tk):
    _, k, n = w_in.shape
    ba_lo = B_OFF + 4 * GDN_WIDTH
    n_packed = n - IN_PACK_SHIFT
    tn = IN_PACK_TILE
    lanes = HEAD_DIM
    return pl.pallas_call(
        functools.partial(_cast_in_kernel, first_shifted=ba_lo // tn),
        grid=(k // tk, n_packed // tn),
        in_specs=[pl.BlockSpec((None, tk, tn), lambda i, j: (layer, i, j)),
                  pl.BlockSpec((None, tk, lanes), lambda i, j: (layer, i, (tn // lanes) * (j + 1)))],
        out_specs=pl.BlockSpec((tk, tn), lambda i, j: (i, j)),
        out_shape=jax.ShapeDtypeStruct((k, n_packed), BF16),
        compiler_params=_cparams("parallel", "parallel"),
        name="cast_in",
    )(w_in, w_in)


def _interleave_gate_up(a):
    lead = a.shape[:-1]
    f = a.shape[-1] // 2
    a = a.reshape(lead + (2, f // FFN_TILE, FFN_TILE))
    a = jnp.swapaxes(a, -3, -2)
    return a.reshape(lead + (2 * f,))


def _rope_tables(s):
    half = HEAD_DIM // 2
    inv_freq = ROPE_THETA ** (-jnp.arange(half, dtype=F32) / half)
    ang = jnp.arange(s, dtype=F32)[:, None] * inv_freq[None, :]
    cos, sin = jnp.cos(ang), jnp.sin(ang)
    return jnp.concatenate([cos, cos], axis=1), jnp.concatenate([-sin, sin], axis=1)


def _swa_kernel(sink_ref, q_ref, kc_ref, kp_ref, vc_ref, vp_ref, cos_ref, sin_ref, cosp_ref, sinp_ref, o_ref):
    n = pl.program_id(1)
    blk = SWA_BLOCK
    cos, sin = cos_ref[...], sin_ref[...]
    cosp, sinp = cosp_ref[...], sinp_ref[...]
    qi = lax.broadcasted_iota(jnp.int32, (blk, 2 * blk), 0)
    kj = lax.broadcasted_iota(jnp.int32, (blk, 2 * blk), 1)
    no_prev = jnp.where(n > 0, 0, blk)
    valid = jnp.where(kj < blk, kj - qi - no_prev, qi - kj + blk + 1) > 0
    scale = HEAD_DIM ** -0.5
    for hk in range(SWA_KV_HEADS):
        sl = slice(hk * HEAD_DIM, (hk + 1) * HEAD_DIM)
        kc = _rope(kc_ref[:, sl].astype(F32), cos, sin)
        kp = _rope(kp_ref[:, sl].astype(F32), cosp, sinp)
        k2 = jnp.concatenate([kp, kc], axis=0).astype(BF16)
        v2 = jnp.concatenate([vp_ref[:, sl], vc_ref[:, sl]], axis=0)
        qs = []
        for g in range(SWA_GROUP):
            hq = hk * SWA_GROUP + g
            qh = _rope(q_ref[:, hq * HEAD_DIM:(hq + 1) * HEAD_DIM].astype(F32), cos, sin) * scale
            qs.append(qh.astype(BF16))
        scores = _mm_nt(jnp.concatenate(qs, axis=0), k2)
        probs, inv_den = [], []
        for g in range(SWA_GROUP):
            sink = sink_ref[hk * SWA_GROUP + g]
            sg = jnp.where(valid, scores[g * blk:(g + 1) * blk], MASKED)
            m = jnp.maximum(jnp.max(sg, axis=1, keepdims=True), sink)
            p = jnp.exp(sg - m)
            den = jnp.sum(p, axis=1, keepdims=True) + jnp.exp(sink - m)
            probs.append(p.astype(BF16))
            inv_den.append(1.0 / den)
        out = jnp.dot(jnp.concatenate(probs, axis=0), v2, preferred_element_type=F32)
        for g in range(SWA_GROUP):
            hq = hk * SWA_GROUP + g
            o_ref[:, hq * HEAD_DIM:(hq + 1) * HEAD_DIM] = (out[g * blk:(g + 1) * blk] * inv_den[g]).astype(o_ref.dtype)


def _swa(proj, sinks, cos_t, sin_t, b, s):
    nb = s // SWA_BLOCK
    kcol = (A_OFF + SWA_WIDTH) // SWA_KV_WIDTH
    vcol = kcol + 1
    cur = lambda bi, n: (bi * nb + n, 0)
    prev_rows = lambda bi, n: bi * nb + jnp.maximum(n - 1, 0)
    return pl.pallas_call(
        _swa_kernel,
        grid=(b, nb),
        in_specs=[pl.BlockSpec(memory_space=pltpu.SMEM),
                  pl.BlockSpec((SWA_BLOCK, SWA_WIDTH), cur),
                  pl.BlockSpec((SWA_BLOCK, SWA_KV_WIDTH), lambda bi, n: (bi * nb + n, kcol)),
                  pl.BlockSpec((SWA_BLOCK, SWA_KV_WIDTH), lambda bi, n: (prev_rows(bi, n), kcol)),
                  pl.BlockSpec((SWA_BLOCK, SWA_KV_WIDTH), lambda bi, n: (bi * nb + n, vcol)),
                  pl.BlockSpec((SWA_BLOCK, SWA_KV_WIDTH), lambda bi, n: (prev_rows(bi, n), vcol)),
                  pl.BlockSpec((SWA_BLOCK, HEAD_DIM), lambda bi, n: (n, 0)),
                  pl.BlockSpec((SWA_BLOCK, HEAD_DIM), lambda bi, n: (n, 0)),
                  pl.BlockSpec((SWA_BLOCK, HEAD_DIM), lambda bi, n: (jnp.maximum(n - 1, 0), 0)),
                  pl.BlockSpec((SWA_BLOCK, HEAD_DIM), lambda bi, n: (jnp.maximum(n - 1, 0), 0))],
        out_specs=pl.BlockSpec((SWA_BLOCK, SWA_WIDTH), cur),
        out_shape=jax.ShapeDtypeStruct((b * s, SWA_WIDTH), BF16),
        compiler_params=_cparams("parallel", "parallel"),
        name="swa",
    )(sinks, proj, proj, proj, proj, proj, cos_t, sin_t, cos_t, sin_t)


def _ret_kernel(q_ref, k_ref, v_ref, g_ref, cos_ref, sin_ref, dm_ref, xi_ref, zeta_ref, gch_ref, o_ref, state_ref):
    @pl.when(pl.program_id(1) == 0)
    def _():
        state_ref[...] = jnp.zeros(state_ref.shape, F32)

    cos, sin = cos_ref[...], sin_ref[...]
    for h in range(RET_HEADS):
        sl = slice(h * HEAD_DIM, (h + 1) * HEAD_DIM)
        q = _rope(q_ref[:, sl].astype(F32), cos, sin)
        k = _rope(k_ref[:, sl].astype(F32), cos, sin) * (HEAD_DIM ** -0.5)
        v = v_ref[:, sl]
        scores = _mm_nt(q, k) * dm_ref[h]
        inner = _mm(scores, v)
        state = state_ref[h]
        cross = _mm(q, state) * xi_ref[h]
        kz_t = jnp.transpose(k * zeta_ref[h])
        state_ref[h] = state * gch_ref[h] + _mm(kz_t, v)
        o = inner + cross
        mu = jnp.mean(o, axis=-1, keepdims=True)
        var = jnp.mean(jnp.square(o - mu), axis=-1, keepdims=True)
        o = (o - mu) * lax.rsqrt(var + NORM_EPS)
        o_ref[:, sl] = (_silu(g_ref[:, sl].astype(F32)) * o).astype(o_ref.dtype)


def _retention(proj, cos_t, sin_t, b, s):
    c = RET_CHUNK
    nc = s // c
    h = RET_HEADS
    log_gamma = jnp.log(1.0 - 2.0 ** (-5.0 - jnp.arange(h, dtype=F32)))
    idx = jnp.arange(c, dtype=F32)
    rel = idx[:, None] - idx[None, :]
    d_mask = jnp.where(rel >= 0, jnp.exp(jnp.maximum(rel, 0.0)[None] * log_gamma[:, None, None]), 0.0)
    xi = jnp.exp((idx + 1.0)[None, :] * log_gamma[:, None])
    zeta = jnp.exp((c - 1.0 - idx)[None, :] * log_gamma[:, None])
    gamma_chunk = jnp.exp(c * log_gamma)
    xi_b = jnp.broadcast_to(xi[:, :, None], (h, c, HEAD_DIM))
    zeta_b = jnp.broadcast_to(zeta[:, :, None], (h, c, HEAD_DIM))
    gch_b = jnp.broadcast_to(gamma_chunk[:, None, None], (h, 1, HEAD_DIM))
    col0 = C_OFF // RET_WIDTH
    whole3 = lambda bi, n: (0, 0, 0)
    row = lambda bi, n: bi * nc + n
    return pl.pallas_call(
        _ret_kernel,
        grid=(b, nc),
        in_specs=[pl.BlockSpec((c, RET_WIDTH), lambda bi, n: (row(bi, n), col0)),
                  pl.BlockSpec((c, RET_WIDTH), lambda bi, n: (row(bi, n), col0 + 1)),
                  pl.BlockSpec((c, RET_WIDTH), lambda bi, n: (row(bi, n), col0 + 2)),
                  pl.BlockSpec((c, RET_WIDTH), lambda bi, n: (row(bi, n), col0 + 3)),
                  pl.BlockSpec((c, HEAD_DIM), lambda bi, n: (n, 0)),
                  pl.BlockSpec((c, HEAD_DIM), lambda bi, n: (n, 0)),
                  pl.BlockSpec((h, c, c), whole3),
                  pl.BlockSpec((h, c, HEAD_DIM), whole3),
                  pl.BlockSpec((h, c, HEAD_DIM), whole3),
                  pl.BlockSpec((h, 1, HEAD_DIM), whole3)],
        out_specs=pl.BlockSpec((c, RET_WIDTH), lambda bi, n: (row(bi, n), 0)),
        out_shape=jax.ShapeDtypeStruct((b * s, RET_WIDTH), BF16),
        scratch_shapes=[pltpu.VMEM((h, HEAD_DIM, HEAD_DIM), F32)],
        compiler_params=_cparams("parallel", "arbitrary"),
        name="retention",
    )(proj, proj, proj, proj, cos_t, sin_t, d_mask, xi_b, zeta_b, gch_b)


def _softplus(x):
    return jnp.maximum(x, 0.0) + jnp.log(1.0 + jnp.exp(-jnp.abs(x)))


def _gdn_kernel(alog_ref, dtb_ref, q_ref, k_ref, v_ref, z_ref, cwq_ref, cwk_ref, cwv_ref, ba_ref, bat_ref, nw_ref,
                o_ref, cbuf_ref, state_ref, *, heads):
    hg = pl.program_id(1)
    n = pl.program_id(2)
    rows, chunk, d = GDN_ROWS, GDN_CHUNK, HEAD_DIM
    hw = heads * d

    @pl.when(n == 0)
    def _():
        state_ref[...] = jnp.zeros(state_ref.shape, F32)
        cbuf_ref[0:8, :] = jnp.zeros((8, cbuf_ref.shape[1]), F32)

    for i, (x_ref, cw_ref) in enumerate(((q_ref, cwq_ref), (k_ref, cwk_ref), (v_ref, cwv_ref))):
        cs = slice(i * hw, (i + 1) * hw)
        cbuf_ref[8:8 + rows, cs] = x_ref[...].astype(F32)
        cw = cw_ref[...]
        y = cw[3:4, :] * cbuf_ref[8:8 + rows, cs]
        for tap in range(1, GDN_CONV):
            y = y + cw[3 - tap:4 - tap, :] * cbuf_ref[8 - tap:8 - tap + rows, cs]
        cbuf_ref[0:8, cs] = cbuf_ref[rows:rows + 8, cs]
        cbuf_ref[8:8 + rows, cs] = _silu(y)

    ri = lax.broadcasted_iota(jnp.int32, (rows, rows), 0)
    ci = lax.broadcasted_iota(jnp.int32, (rows, rows), 1)

    def same(blk):
        sh = blk.bit_length() - 1
        return jnp.right_shift(ri, sh) == jnp.right_shift(ci, sh)

    causal = jnp.logical_and(same(chunk), ci <= ri)
    causal_t = jnp.logical_and(same(chunk), ri <= ci)
    strict = jnp.logical_and(same(chunk), ci < ri)
    lane = lax.broadcasted_iota(jnp.int32, (rows, BA_WIDTH), 1)
    ba = ba_ref[...]

    hr = range(heads)
    each = lambda f, *ls: [f(*a) for a in zip(*ls)]
    q = [cbuf_ref[8:8 + rows, hh * d:(hh + 1) * d] for hh in hr]
    k = [cbuf_ref[8:8 + rows, hw + hh * d:hw + (hh + 1) * d] for hh in hr]
    v = [cbuf_ref[8:8 + rows, 2 * hw + hh * d:2 * hw + (hh + 1) * d] for hh in hr]
    q = each(lambda t: t * lax.rsqrt(jnp.sum(t * t, axis=-1, keepdims=True) + NORM_EPS) * (d ** -0.5), q)
    k = each(lambda t: t * lax.rsqrt(jnp.sum(t * t, axis=-1, keepdims=True) + NORM_EPS), k)
    kb = each(lambda t: t.astype(BF16), k)

    head = [hg * heads + hh for hh in hr]
    a_neg = [-jnp.exp(alog_ref[h]) for h in head]
    dtb = [dtb_ref[h] for h in head]
    beta_c = [jax.nn.sigmoid(jnp.sum(jnp.where(lane == h, ba, 0.0), axis=1, keepdims=True)) for h in head]
    alpha_c = [jnp.sum(jnp.where(lane == GDN_HEADS + h, ba, 0.0), axis=1, keepdims=True) for h in head]
    alpha_r = [bat_ref[pl.ds(GDN_HEADS + h, 1), :] for h in head]
    g_c = each(lambda a, x, t: a * _softplus(x + t), a_neg, alpha_c, dtb)
    g_r = each(lambda a, x, t: a * _softplus(x + t), a_neg, alpha_r, dtb)
    gc_c = each(lambda g: jnp.sum(jnp.where(causal, g, 0.0), axis=1, keepdims=True), g_r)
    gc_r = each(lambda g: jnp.sum(jnp.where(causal_t, g, 0.0), axis=0, keepdims=True), g_c)
    decay = each(lambda c, r: jnp.exp(jnp.where(causal, c - r, MASKED)), gc_c, gc_r)

    low = each(lambda bc, t, dc: jnp.where(strict, bc * _mm_nt(t, t) * dc, 0.0), beta_c, kb, decay)
    dg = each(lambda t: jnp.where(same(16), t, 0.0), low)
    d2 = each(lambda t: _mm(t, t), dg)
    d4 = each(lambda t: _mm(t, t), d2)
    e1 = each(lambda a, b2: b2 - a - _mm(a, b2), dg, d2)
    d8 = each(lambda t: _mm(t, t), d4)
    e2 = each(lambda a, b2: a + b2 + _mm(a, b2), d4, d8)
    e = each(lambda a, b2: a + b2 + _mm(a, b2), e1, e2)
    for blk in (16, 32):
        off_mask = jnp.logical_and(same(2 * blk), jnp.logical_not(same(blk)))
        off = each(lambda t: jnp.where(off_mask, t, 0.0), low)
        y = each(lambda o_, e_: o_ + _mm(o_, e_), off, e)
        e = each(lambda e_, y_: e_ - y_ - _mm(e_, y_), e, y)

    rhs = each(lambda v_, k_, bc, gc: jnp.concatenate([v_ * bc, k_ * (bc * jnp.exp(gc))], axis=1), v, k, beta_c, gc_c)
    sol = each(lambda r, e_: r + _mm(e_, r), rhs, e)
    qk = each(lambda q_, t, dc: _mm_nt(q_, t) * dc, q, kb, decay)
    qg = each(lambda q_, gc: q_ * jnp.exp(gc), q, gc_c)

    state = [state_ref[hh] for hh in hr]
    v_new = [[] for _ in hr]
    cross = [[] for _ in hr]
    for c in range(rows // chunk):
        r0 = c * chunk
        rs = slice(r0, r0 + chunk)
        both = each(lambda s_, g_, st: _mm(jnp.concatenate([s_[rs, d:], g_[rs]], axis=0), st), sol, qg, state)
        vn = each(lambda s_, bt: s_[rs, :d] - bt[:chunk], sol, both)
        g_last = [gc[r0 + chunk - 1:r0 + chunk, :] for gc in gc_c]
        kd_t = each(lambda k_, gl, gc: jnp.transpose(k_[rs] * jnp.exp(gl - gc[rs])), k, g_last, gc_c)
        state = each(lambda st, gl, kt, vn_: st * jnp.exp(gl) + _mm(kt, vn_), state, g_last, kd_t, vn)
        for hh in hr:
            v_new[hh].append(vn[hh])
            cross[hh].append(both[hh][chunk:])
    for hh in hr:
        hs = slice(hh * d, (hh + 1) * d)
        state_ref[hh] = state[hh]
        o = jnp.concatenate(cross[hh], axis=0) + _mm(qk[hh], jnp.concatenate(v_new[hh], axis=0))
        o = o * lax.rsqrt(jnp.mean(o * o, axis=-1, keepdims=True) + NORM_EPS) * nw_ref[...]
        o_ref[:, hs] = (o * _silu(z_ref[:, hs].astype(F32))).astype(o_ref.dtype)


def _gdn(proj, ba, conv_w, a_log, dt_bias, norm_w, b, s):
    rows, heads = GDN_ROWS, GDN_HEADS_PER_STEP
    nt = s // rows
    hw = heads * HEAD_DIM
    groups = GDN_HEADS // heads
    col0 = B_OFF // hw
    gcol = GDN_WIDTH // hw
    row = lambda bi, hg, n: bi * nt + n
    smem = pl.BlockSpec(memory_space=pltpu.SMEM)

    def xspec(i):
        return pl.BlockSpec((rows, hw), lambda bi, hg, n: (row(bi, hg, n), col0 + i * gcol + hg))

    def cwspec(i):
        return pl.BlockSpec((GDN_CONV, hw), lambda bi, hg, n: (0, i * gcol + hg))

    return pl.pallas_call(
        functools.partial(_gdn_kernel, heads=heads),
        grid=(b, groups, nt),
        in_specs=[smem, smem, xspec(0), xspec(1), xspec(2), xspec(3), cwspec(0), cwspec(1), cwspec(2),
                  pl.BlockSpec((rows, BA_WIDTH), lambda bi, hg, n: (row(bi, hg, n), 0)),
                  pl.BlockSpec((2 * GDN_HEADS, rows), lambda bi, hg, n: (0, row(bi, hg, n))),
                  pl.BlockSpec((1, HEAD_DIM), lambda bi, hg, n: (0, 0))],
        out_specs=pl.BlockSpec((rows, hw), lambda bi, hg, n: (row(bi, hg, n), hg)),
        out_shape=jax.ShapeDtypeStruct((b * s, GDN_WIDTH), BF16),
        scratch_shapes=[pltpu.VMEM((rows + 8, 3 * hw), F32),
                        pltpu.VMEM((heads, HEAD_DIM, HEAD_DIM), F32)],
        compiler_params=_cparams("parallel", "parallel", "arbitrary"),
        name="gdn",
    )(a_log, dt_bias, proj, proj, proj, proj, conv_w, conv_w, conv_w, ba,
      jnp.transpose(ba[:, :2 * GDN_HEADS]), norm_w.reshape(1, HEAD_DIM))


def kernel(x, ln_mix_w, ln_ffn_w, w_in, gdn_conv_w, gdn_a_log, gdn_dt_bias, gdn_norm_w, attn_sinks,
           w_branch, w_out, w_up, ffn_conv_w, ffn_conv_b, w_down, ln_final_w):
    b, s, d = x.shape
    t = b * s
    depth = w_in.shape[0]
    tm = min(ROW_TILE, s)
    xf = x.reshape(t, d)
    cos_t, sin_t = _rope_tables(s)
    w_in_bf = w_in.astype(BF16)
    for layer in range(depth):
        w_in_p = _cast_pack_in(w_in_bf, layer, min(2048, d))
        w_up_p = _cast_interleave_up(w_up, layer)
        conv_w_p = _interleave_gate_up(ffn_conv_w[layer])
        conv_b_p = _interleave_gate_up(ffn_conv_b[layer]).reshape(1, -1)

        h = _rmsnorm(xf, ln_mix_w[layer], BF16)
        proj = _matmul(h, w_in_p, BF16, tm, 1024 if w_in_p.shape[1] % 1024 == 0 else 512, "in_proj")
        ba = _gate_proj(h, w_in_bf, layer, tm)
        out_a = _swa(proj, attn_sinks[layer], cos_t, sin_t, b, s)
        out_b = _gdn(proj, ba, gdn_conv_w[layer], gdn_a_log[layer], gdn_dt_bias[layer], gdn_norm_w[layer], b, s)
        out_c = _retention(proj, cos_t, sin_t, b, s)
        merged = _branch_merge(out_a, out_b, out_c, _cast_bf16(w_branch, layer, 256), proj, d, tm, 512)
        xf = _matmul_residual(merged, _cast_bf16(w_out, layer, 256), xf, tm, 512, 1, "out_proj")

        h = _rmsnorm(xf, ln_ffn_w[layer], BF16)
        act = _up_glu(h, w_up_p, conv_w_p, conv_b_p, s, tm)
        xf = _matmul_residual(act, _cast_bf16(w_down, layer, 256), xf, min(tm, 512), 512, 1, "down_proj")
    out = _rmsnorm(xf, ln_final_w, x.dtype)
    return out.reshape(b, s, d)
```

```python
import functools

import jax
import jax.numpy as jnp
from jax import lax
from jax.experimental import pallas as pl
from jax.experimental.pallas import tpu as pltpu

HEAD_DIM = 128
NORM_EPS = 1e-6
ROPE_THETA = 10000.0
SWA_Q_HEADS = 16
SWA_KV_HEADS = 4
SWA_GROUP = SWA_Q_HEADS // SWA_KV_HEADS
SWA_BLOCK = 128
GDN_HEADS = 8
GDN_CONV = 4
GDN_CHUNK = 64
GDN_ROWS = 256
GDN_HEADS_PER_STEP = 4
RET_HEADS = 8
RET_CHUNK = 128
FFN_CONV = 3
N_BRANCHES = 3

SWA_WIDTH = SWA_Q_HEADS * HEAD_DIM
SWA_KV_WIDTH = SWA_KV_HEADS * HEAD_DIM
GDN_WIDTH = GDN_HEADS * HEAD_DIM
RET_WIDTH = RET_HEADS * HEAD_DIM

A_OFF = 0
B_OFF = SWA_WIDTH + 2 * SWA_KV_WIDTH
C_OFF = B_OFF + 4 * GDN_WIDTH
G_OFF = C_OFF + 4 * RET_WIDTH
BA_WIDTH = 128

VMEM_LIMIT_BYTES = 52 * 1024 * 1024
FFN_TILE = 256
ROW_TILE = 1024
MASKED = -1e30

F32 = jnp.float32
BF16 = jnp.bfloat16
NT_DIMS = (((1,), (1,)), ((), ()))


def _cparams(*sem):
    return pltpu.CompilerParams(dimension_semantics=sem, vmem_limit_bytes=VMEM_LIMIT_BYTES)


def _mm(a, b):
    return jnp.dot(a.astype(BF16), b.astype(BF16), preferred_element_type=F32)


def _mm_nt(a, b):
    return lax.dot_general(a.astype(BF16), b.astype(BF16), NT_DIMS, preferred_element_type=F32)


def _silu(x):
    return x * jax.nn.sigmoid(x)


def _rope(x, cos, sin_signed):
    return x * cos + pltpu.roll(x, HEAD_DIM // 2, 1) * sin_signed


def _rmsnorm_kernel(x_ref, w_ref, o_ref):
    x = x_ref[...]
    y = x * lax.rsqrt(jnp.mean(x * x, axis=-1, keepdims=True) + NORM_EPS)
    o_ref[...] = (y * w_ref[...]).astype(o_ref.dtype)


def _rmsnorm(x, w, out_dtype, tm=256):
    t, d = x.shape
    return pl.pallas_call(
        _rmsnorm_kernel,
        grid=(t // tm,),
        in_specs=[pl.BlockSpec((tm, d), lambda i: (i, 0)),
                  pl.BlockSpec((1, d), lambda i: (0, 0))],
        out_specs=pl.BlockSpec((tm, d), lambda i: (i, 0)),
        out_shape=jax.ShapeDtypeStruct((t, d), out_dtype),
        compiler_params=_cparams("parallel"),
        name="rmsnorm",
    )(x, w.reshape(1, d))


def _mm_kernel(x_ref, w_ref, o_ref):
    o_ref[...] = jnp.dot(x_ref[...], w_ref[...],
                         preferred_element_type=F32).astype(o_ref.dtype)


def _matmul(x, w, out_dtype, tm, tn, name):
    m, k = x.shape
    n = w.shape[1]
    return pl.pallas_call(
        _mm_kernel,
        grid=(m // tm, n // tn),
        in_specs=[pl.BlockSpec((tm, k), lambda i, j: (i, 0)),
                  pl.BlockSpec((k, tn), lambda i, j: (0, j))],
        out_specs=pl.BlockSpec((tm, tn), lambda i, j: (i, j)),
        out_shape=jax.ShapeDtypeStruct((m, n), out_dtype),
        compiler_params=_cparams("parallel", "arbitrary"),
        name=name,
    )(x, w)


def _gate_proj_kernel(x_ref, w_ref, o_ref):
    o_ref[...] = jnp.dot(x_ref[...], w_ref[...].astype(BF16), preferred_element_type=F32)


def _gate_proj(x, w_in, layer, tm):
    m, k = x.shape
    col_blk = (B_OFF + 4 * GDN_WIDTH) // BA_WIDTH
    return pl.pallas_call(
        _gate_proj_kernel,
        grid=(m // tm,),
        in_specs=[pl.BlockSpec((tm, k), lambda i: (i, 0)),
                  pl.BlockSpec((None, k, BA_WIDTH), lambda i: (layer, 0, col_blk))],
        out_specs=pl.BlockSpec((tm, BA_WIDTH), lambda i: (i, 0)),
        out_shape=jax.ShapeDtypeStruct((m, BA_WIDTH), F32),
        compiler_params=_cparams("parallel"),
        name="in_proj_gates",
    )(x, w_in)


def _mm_res_kernel(x_ref, w_ref, r_ref, o_ref, acc_ref, *, nk):
    kk = pl.program_id(2)
    part = jnp.dot(x_ref[...], w_ref[...], preferred_element_type=F32)
    if nk == 1:
        o_ref[...] = r_ref[...] + part
    else:
        @pl.when(kk == 0)
        def _():
            acc_ref[...] = part

        @pl.when(jnp.logical_and(kk > 0, kk < nk - 1))
        def _():
            acc_ref[...] += part

        @pl.when(kk == nk - 1)
        def _():
            o_ref[...] = r_ref[...] + (acc_ref[...] + part)


def _matmul_residual(x, w, res, tm, tn, nk, name):
    m, k = x.shape
    n = w.shape[1]
    tk = k // nk
    return pl.pallas_call(
        functools.partial(_mm_res_kernel, nk=nk),
        grid=(m // tm, n // tn, nk),
        in_specs=[pl.BlockSpec((tm, tk), lambda i, j, kk: (i, kk)),
                  pl.BlockSpec((tk, tn), lambda i, j, kk: (kk, j)),
                  pl.BlockSpec((tm, tn), lambda i, j, kk: (i, j))],
        out_specs=pl.BlockSpec((tm, tn), lambda i, j, kk: (i, j)),
        out_shape=jax.ShapeDtypeStruct((m, n), F32),
        scratch_shapes=[pltpu.VMEM((tm, tn), F32)],
        compiler_params=_cparams("parallel", "arbitrary", "arbitrary"),
        name=name,
    )(x, w, res)


def _merge_kernel(a_ref, b_ref, c_ref, wa_ref, wb_ref, wc_ref, ga_ref, gb_ref, gc_ref, o_ref):
    def branch(x_ref, w_ref, g_ref):
        p = jnp.dot(x_ref[...], w_ref[...], preferred_element_type=F32)
        return jax.nn.sigmoid(g_ref[...].astype(F32)) * p

    merged = branch(a_ref, wa_ref, ga_ref) + branch(b_ref, wb_ref, gb_ref) + branch(c_ref, wc_ref, gc_ref)
    o_ref[...] = merged.astype(o_ref.dtype)


def _branch_merge(out_a, out_b, out_c, w_branch, proj, d_model, tm, tn):
    t = out_a.shape[0]
    gate_blk = G_OFF // tn
    d_blk = d_model // tn
    b_row = SWA_WIDTH // GDN_WIDTH
    return pl.pallas_call(
        _merge_kernel,
        grid=(t // tm, d_model // tn),
        in_specs=[pl.BlockSpec((tm, SWA_WIDTH), lambda i, j: (i, 0)),
                  pl.BlockSpec((tm, GDN_WIDTH), lambda i, j: (i, 0)),
                  pl.BlockSpec((tm, RET_WIDTH), lambda i, j: (i, 0)),
                  pl.BlockSpec((SWA_WIDTH, tn), lambda i, j: (0, j)),
                  pl.BlockSpec((GDN_WIDTH, tn), lambda i, j: (b_row, j)),
                  pl.BlockSpec((RET_WIDTH, tn), lambda i, j: (b_row + 1, j)),
                  pl.BlockSpec((tm, tn), lambda i, j: (i, gate_blk + j)),
                  pl.BlockSpec((tm, tn), lambda i, j: (i, gate_blk + d_blk + j)),
                  pl.BlockSpec((tm, tn), lambda i, j: (i, gate_blk + 2 * d_blk + j))],
        out_specs=pl.BlockSpec((tm, tn), lambda i, j: (i, j)),
        out_shape=jax.ShapeDtypeStruct((t, d_model), BF16),
        compiler_params=_cparams("parallel", "arbitrary"),
        name="branch_merge",
    )(out_a, out_b, out_c, w_branch, w_branch, w_branch, proj, proj, proj)


def _upglu_kernel(x_ref, wg_ref, wu_ref, cw_ref, cb_ref, o_ref, u_ref, w_ref, *, tm, tiles_per_seq):
    mi = pl.program_id(1)

    @pl.when(mi == 0)
    def _():
        w_ref[:, :FFN_TILE] = wg_ref[...].astype(BF16)
        w_ref[:, FFN_TILE:] = wu_ref[...].astype(BF16)

    @pl.when(mi % tiles_per_seq == 0)
    def _():
        u_ref[0:8, :] = jnp.zeros((8, u_ref.shape[1]), F32)

    u_ref[8:8 + tm, :] = jnp.dot(x_ref[...], w_ref[...], preferred_element_type=F32)
    cw = cw_ref[...]
    y = cb_ref[...] + cw[2:3, :] * u_ref[8:8 + tm, :]
    y = y + cw[1:2, :] * u_ref[7:7 + tm, :]
    y = y + cw[0:1, :] * u_ref[6:6 + tm, :]
    u_ref[0:8, :] = u_ref[tm:tm + 8, :]
    o_ref[...] = (_silu(y[:, :FFN_TILE]) * y[:, FFN_TILE:]).astype(o_ref.dtype)


def _up_glu(h, w_up, layer, conv_w_p, conv_b_p, seq, tm):
    t, d = h.shape
    n2 = w_up.shape[2]
    nblk = n2 // (2 * FFN_TILE)
    return pl.pallas_call(
        functools.partial(_upglu_kernel, tm=tm, tiles_per_seq=seq // tm),
        grid=(nblk, t // tm),
        in_specs=[pl.BlockSpec((tm, d), lambda j, i: (i, 0)),
                  pl.BlockSpec((None, d, FFN_TILE), lambda j, i: (layer, 0, j)),
                  pl.BlockSpec((None, d, FFN_TILE), lambda j, i: (layer, 0, nblk + j)),
                  pl.BlockSpec((FFN_CONV, 2 * FFN_TILE), lambda j, i: (0, j)),
                  pl.BlockSpec((1, 2 * FFN_TILE), lambda j, i: (0, j))],
        out_specs=pl.BlockSpec((tm, FFN_TILE), lambda j, i: (i, j)),
        out_shape=jax.ShapeDtypeStruct((t, nblk * FFN_TILE), BF16),
        scratch_shapes=[pltpu.VMEM((tm + 8, 2 * FFN_TILE), F32),
                        pltpu.VMEM((d, 2 * FFN_TILE), BF16)],
        compiler_params=_cparams("parallel", "arbitrary"),
        name="up_conv_glu",
    )(h, w_up, w_up, conv_w_p, conv_b_p)


def _cast_kernel(w_ref, o_ref):
    o_ref[...] = w_ref[...].astype(o_ref.dtype)


def _cast_bf16(w, layer, tk):
    _, k, n = w.shape
    return pl.pallas_call(
        _cast_kernel,
        grid=(k // tk,),
        in_specs=[pl.BlockSpec((None, tk, n), lambda i: (layer, i, 0))],
        out_specs=pl.BlockSpec((tk, n), lambda i: (i, 0)),
        out_shape=jax.ShapeDtypeStruct((k, n), BF16),
        compiler_params=_cparams("parallel"),
        name="cast_bf16",
    )(w)


IN_PACK_TILE = 512
IN_PACK_SHIFT = 2 * GDN_HEADS


def _cast_in_kernel(a_ref, b_ref, o_ref, *, first_shifted):
    j = pl.program_id(1)

    @pl.when(j < first_shifted)
    def _():
        o_ref[...] = a_ref[...].astype(o_ref.dtype)

    @pl.when(j >= first_shifted)
    def _():
        shifted = jnp.concatenate([a_ref[:, IN_PACK_SHIFT:], b_ref[:, :IN_PACK_SHIFT]], axis=1)
        o_ref[...] = shifted.astype(o_ref.dtype)


def _cast_pack_in(w_in, layer, tk):
    _, k, n = w_in.shape
    ba_lo = B_OFF + 4 * GDN_WIDTH
    n_packed = n - IN_PACK_SHIFT
    tn = IN_PACK_TILE
    lanes = HEAD_DIM
    return pl.pallas_call(
        functools.partial(_cast_in_kernel, first_shifted=ba_lo // tn),
        grid=(k // tk, n_packed // tn),
        in_specs=[pl.BlockSpec((None, tk, tn), lambda i, j: (layer, i, j)),
                  pl.BlockSpec((None, tk, lanes), lambda i, j: (layer, i, (tn // lanes) * (j + 1)))],
        out_specs=pl.BlockSpec((tk, tn), lambda i, j: (i, j)),
        out_shape=jax.ShapeDtypeStruct((k, n_packed), BF16),
        compiler_params=_cparams("parallel", "parallel"),
        name="cast_in",
    )(w_in, w_in)


def _interleave_gate_up(a):
    lead = a.shape[:-1]
    f = a.shape[-1] // 2
    a = a.reshape(lead + (2, f // FFN_TILE, FFN_TILE))
    a = jnp.swapaxes(a, -3, -2)
    return a.reshape(lead + (2 * f,))


def _rope_tables(s):
    half = HEAD_DIM // 2
    inv_freq = ROPE_THETA ** (-jnp.arange(half, dtype=F32) / half)
    ang = jnp.arange(s, dtype=F32)[:, None] * inv_freq[None, :]
    cos, sin = jnp.cos(ang), jnp.sin(ang)
    return jnp.concatenate([cos, cos], axis=1), jnp.concatenate([-sin, sin], axis=1)


def _swa_kernel(sink_ref, q_ref, kc_ref, kp_ref, vc_ref, vp_ref, cos_ref, sin_ref, cosp_ref, sinp_ref, o_ref):
    n = pl.program_id(1)
    blk = SWA_BLOCK
    cos, sin = cos_ref[...], sin_ref[...]
    cosp, sinp = cosp_ref[...], sinp_ref[...]
    qi = lax.broadcasted_iota(jnp.int32, (blk, 2 * blk), 0)
    kj = lax.broadcasted_iota(jnp.int32, (blk, 2 * blk), 1)
    no_prev = jnp.where(n > 0, 0, blk)
    valid = jnp.where(kj < blk, kj - qi - no_prev, qi - kj + blk + 1) > 0
    scale = HEAD_DIM ** -0.5
    for hk in range(SWA_KV_HEADS):
        sl = slice(hk * HEAD_DIM, (hk + 1) * HEAD_DIM)
        kc = _rope(kc_ref[:, sl].astype(F32), cos, sin)
        kp = _rope(kp_ref[:, sl].astype(F32), cosp, sinp)
        k2 = jnp.concatenate([kp, kc], axis=0).astype(BF16)
        v2 = jnp.concatenate([vp_ref[:, sl], vc_ref[:, sl]], axis=0)
        qs = []
        for g in range(SWA_GROUP):
            hq = hk * SWA_GROUP + g
            qh = _rope(q_ref[:, hq * HEAD_DIM:(hq + 1) * HEAD_DIM].astype(F32), cos, sin) * scale
            qs.append(qh.astype(BF16))
        scores = _mm_nt(jnp.concatenate(qs, axis=0), k2)
        probs, inv_den = [], []
        for g in range(SWA_GROUP):
            sink = sink_ref[hk * SWA_GROUP + g]
            sg = jnp.where(valid, scores[g * blk:(g + 1) * blk], MASKED)
            m = jnp.maximum(jnp.max(sg, axis=1, keepdims=True), sink)
            p = jnp.exp(sg - m)
            den = jnp.sum(p, axis=1, keepdims=True) + jnp.exp(sink - m)
            probs.append(p.astype(BF16))
            inv_den.append(1.0 / den)
        out = jnp.dot(jnp.concatenate(probs, axis=0), v2, preferred_element_type=F32)
        for g in range(SWA_GROUP):
            hq = hk * SWA_GROUP + g
            o_ref[:, hq * HEAD_DIM:(hq + 1) * HEAD_DIM] = (out[g * blk:(g + 1) * blk] * inv_den[g]).astype(o_ref.dtype)


def _swa(proj, sinks, cos_t, sin_t, b, s):
    nb = s // SWA_BLOCK
    kcol = (A_OFF + SWA_WIDTH) // SWA_KV_WIDTH
    vcol = kcol + 1
    cur = lambda bi, n: (bi * nb + n, 0)
    prev_rows = lambda bi, n: bi * nb + jnp.maximum(n - 1, 0)
    return pl.pallas_call(
        _swa_kernel,
        grid=(b, nb),
        in_specs=[pl.BlockSpec(memory_space=pltpu.SMEM),
                  pl.BlockSpec((SWA_BLOCK, SWA_WIDTH), cur),
                  pl.BlockSpec((SWA_BLOCK, SWA_KV_WIDTH), lambda bi, n: (bi * nb + n, kcol)),
                  pl.BlockSpec((SWA_BLOCK, SWA_KV_WIDTH), lambda bi, n: (prev_rows(bi, n), kcol)),
                  pl.BlockSpec((SWA_BLOCK, SWA_KV_WIDTH), lambda bi, n: (bi * nb + n, vcol)),
                  pl.BlockSpec((SWA_BLOCK, SWA_KV_WIDTH), lambda bi, n: (prev_rows(bi, n), vcol)),
                  pl.BlockSpec((SWA_BLOCK, HEAD_DIM), lambda bi, n: (n, 0)),
                  pl.BlockSpec((SWA_BLOCK, HEAD_DIM), lambda bi, n: (n, 0)),
                  pl.BlockSpec((SWA_BLOCK, HEAD_DIM), lambda bi, n: (jnp.maximum(n - 1, 0), 0)),
                  pl.BlockSpec((SWA_BLOCK, HEAD_DIM), lambda bi, n: (jnp.maximum(n - 1, 0), 0))],
        out_specs=pl.BlockSpec((SWA_BLOCK, SWA_WIDTH), cur),
        out_shape=jax.ShapeDtypeStruct((b * s, SWA_WIDTH), BF16),
        compiler_params=_cparams("parallel", "parallel"),
        name="swa",
    )(sinks, proj, proj, proj, proj, proj, cos_t, sin_t, cos_t, sin_t)


def _ret_kernel(q_ref, k_ref, v_ref, g_ref, cos_ref, sin_ref, dm_ref, xi_ref, zeta_ref, gch_ref, o_ref, state_ref):
    @pl.when(pl.program_id(1) == 0)
    def _():
        state_ref[...] = jnp.zeros(state_ref.shape, F32)

    cos, sin = cos_ref[...], sin_ref[...]
    for h in range(RET_HEADS):
        sl = slice(h * HEAD_DIM, (h + 1) * HEAD_DIM)
        q = _rope(q_ref[:, sl].astype(F32), cos, sin)
        k = _rope(k_ref[:, sl].astype(F32), cos, sin) * (HEAD_DIM ** -0.5)
        v = v_ref[:, sl]
        scores = _mm_nt(q, k) * dm_ref[h]
        inner = _mm(scores, v)
        state = state_ref[h]
        cross = _mm(q, state) * xi_ref[h]
        kz_t = jnp.transpose(k * zeta_ref[h])
        state_ref[h] = state * gch_ref[h] + _mm(kz_t, v)
        o = inner + cross
        mu = jnp.mean(o, axis=-1, keepdims=True)
        var = jnp.mean(jnp.square(o - mu), axis=-1, keepdims=True)
        o = (o - mu) * lax.rsqrt(var + NORM_EPS)
        o_ref[:, sl] = (_silu(g_ref[:, sl].astype(F32)) * o).astype(o_ref.dtype)


def _retention(proj, cos_t, sin_t, b, s):
    c = RET_CHUNK
    nc = s // c
    h = RET_HEADS
    log_gamma = jnp.log(1.0 - 2.0 ** (-5.0 - jnp.arange(h, dtype=F32)))
    idx = jnp.arange(c, dtype=F32)
    rel = idx[:, None] - idx[None, :]
    d_mask = jnp.where(rel >= 0, jnp.exp(jnp.maximum(rel, 0.0)[None] * log_gamma[:, None, None]), 0.0)
    xi = jnp.exp((idx + 1.0)[None, :] * log_gamma[:, None])
    zeta = jnp.exp((c - 1.0 - idx)[None, :] * log_gamma[:, None])
    gamma_chunk = jnp.exp(c * log_gamma)
    xi_b = jnp.broadcast_to(xi[:, :, None], (h, c, HEAD_DIM))
    zeta_b = jnp.broadcast_to(zeta[:, :, None], (h, c, HEAD_DIM))
    gch_b = jnp.broadcast_to(gamma_chunk[:, None, None], (h, 1, HEAD_DIM))
    col0 = C_OFF // RET_WIDTH
    whole3 = lambda bi, n: (0, 0, 0)
    row = lambda bi, n: bi * nc + n
    return pl.pallas_call(
        _ret_kernel,
        grid=(b, nc),
        in_specs=[pl.BlockSpec((c, RET_WIDTH), lambda bi, n: (row(bi, n), col0)),
                  pl.BlockSpec((c, RET_WIDTH), lambda bi, n: (row(bi, n), col0 + 1)),
                  pl.BlockSpec((c, RET_WIDTH), lambda bi, n: (row(bi, n), col0 + 2)),
                  pl.BlockSpec((c, RET_WIDTH), lambda bi, n: (row(bi, n), col0 + 3)),
                  pl.BlockSpec((c, HEAD_DIM), lambda bi, n: (n, 0)),
                  pl.BlockSpec((c, HEAD_DIM), lambda bi, n: (n, 0)),
                  pl.BlockSpec((h, c, c), whole3),
                  pl.BlockSpec((h, c, HEAD_DIM), whole3),
                  pl.BlockSpec((h, c, HEAD_DIM), whole3),
                  pl.BlockSpec((h, 1, HEAD_DIM), whole3)],
        out_specs=pl.BlockSpec((c, RET_WIDTH), lambda bi, n: (row(bi, n), 0)),
        out_shape=jax.ShapeDtypeStruct((b * s, RET_WIDTH), BF16),
        scratch_shapes=[pltpu.VMEM((h, HEAD_DIM, HEAD_DIM), F32)],
        compiler_params=_cparams("parallel", "arbitrary"),
        name="retention",
    )(proj, proj, proj, proj, cos_t, sin_t, d_mask, xi_b, zeta_b, gch_b)


def _softplus(x):
    return jnp.maximum(x, 0.0) + jnp.log(1.0 + jnp.exp(-jnp.abs(x)))


def _gdn_kernel(alog_ref, dtb_ref, q_ref, k_ref, v_ref, z_ref, cwq_ref, cwk_ref, cwv_ref, ba_ref, bat_ref, nw_ref,
                o_ref, cbuf_ref, state_ref, *, heads):
    hg = pl.program_id(1)
    n = pl.program_id(2)
    rows, chunk, d = GDN_ROWS, GDN_CHUNK, HEAD_DIM
    hw = heads * d

    @pl.when(n == 0)
    def _():
        state_ref[...] = jnp.zeros(state_ref.shape, F32)
        cbuf_ref[0:8, :] = jnp.zeros((8, cbuf_ref.shape[1]), F32)

    for i, (x_ref, cw_ref) in enumerate(((q_ref, cwq_ref), (k_ref, cwk_ref), (v_ref, cwv_ref))):
        cs = slice(i * hw, (i + 1) * hw)
        cbuf_ref[8:8 + rows, cs] = x_ref[...].astype(F32)
        cw = cw_ref[...]
        y = cw[3:4, :] * cbuf_ref[8:8 + rows, cs]
        for tap in range(1, GDN_CONV):
            y = y + cw[3 - tap:4 - tap, :] * cbuf_ref[8 - tap:8 - tap + rows, cs]
        cbuf_ref[0:8, cs] = cbuf_ref[rows:rows + 8, cs]
        cbuf_ref[8:8 + rows, cs] = _silu(y)

    ri = lax.broadcasted_iota(jnp.int32, (rows, rows), 0)
    ci = lax.broadcasted_iota(jnp.int32, (rows, rows), 1)

    def same(blk):
        sh = blk.bit_length() - 1
        return jnp.right_shift(ri, sh) == jnp.right_shift(ci, sh)

    causal = jnp.logical_and(same(chunk), ci <= ri)
    causal_t = jnp.logical_and(same(chunk), ri <= ci)
    strict = jnp.logical_and(same(chunk), ci < ri)
    lane = lax.broadcasted_iota(jnp.int32, (rows, BA_WIDTH), 1)
    ba = ba_ref[...]

    hr = range(heads)
    each = lambda f, *ls: [f(*a) for a in zip(*ls)]
    q = [cbuf_ref[8:8 + rows, hh * d:(hh + 1) * d] for hh in hr]
    k = [cbuf_ref[8:8 + rows, hw + hh * d:hw + (hh + 1) * d] for hh in hr]
    v = [cbuf_ref[8:8 + rows, 2 * hw + hh * d:2 * hw + (hh + 1) * d] for hh in hr]
    q = each(lambda t: t * lax.rsqrt(jnp.sum(t * t, axis=-1, keepdims=True) + NORM_EPS) * (d ** -0.5), q)
    k = each(lambda t: t * lax.rsqrt(jnp.sum(t * t, axis=-1, keepdims=True) + NORM_EPS), k)
    kb = each(lambda t: t.astype(BF16), k)

    head = [hg * heads + hh for hh in hr]
    a_neg = [-jnp.exp(alog_ref[h]) for h in head]
    dtb = [dtb_ref[h] for h in head]
    beta_c = [jax.nn.sigmoid(jnp.sum(jnp.where(lane == h, ba, 0.0), axis=1, keepdims=True)) for h in head]
    alpha_c = [jnp.sum(jnp.where(lane == GDN_HEADS + h, ba, 0.0), axis=1, keepdims=True) for h in head]
    alpha_r = [bat_ref[pl.ds(GDN_HEADS + h, 1), :] for h in head]
    g_c = each(lambda a, x, t: a * _softplus(x + t), a_neg, alpha_c, dtb)
    g_r = each(lambda a, x, t: a * _softplus(x + t), a_neg, alpha_r, dtb)
    gc_c = each(lambda g: jnp.sum(jnp.where(causal, g, 0.0), axis=1, keepdims=True), g_r)
    gc_r = each(lambda g: jnp.sum(jnp.where(causal_t, g, 0.0), axis=0, keepdims=True), g_c)
    decay = each(lambda c, r: jnp.exp(jnp.where(causal, c - r, MASKED)), gc_c, gc_r)

    low = each(lambda bc, t, dc: jnp.where(strict, bc * _mm_nt(t, t) * dc, 0.0), beta_c, kb, decay)
    dg = each(lambda t: jnp.where(same(16), t, 0.0), low)
    d2 = each(lambda t: _mm(t, t), dg)
    d4 = each(lambda t: _mm(t, t), d2)
    e1 = each(lambda a, b2: b2 - a - _mm(a, b2), dg, d2)
    d8 = each(lambda t: _mm(t, t), d4)
    e2 = each(lambda a, b2: a + b2 + _mm(a, b2), d4, d8)
    e = each(lambda a, b2: a + b2 + _mm(a, b2), e1, e2)
    for blk in (16, 32):
        off_mask = jnp.logical_and(same(2 * blk), jnp.logical_not(same(blk)))
        off = each(lambda t: jnp.where(off_mask, t, 0.0), low)
        y = each(lambda o_, e_: o_ + _mm(o_, e_), off, e)
        e = each(lambda e_, y_: e_ - y_ - _mm(e_, y_), e, y)

    rhs = each(lambda v_, k_, bc, gc: jnp.concatenate([v_ * bc, k_ * (bc * jnp.exp(gc))], axis=1), v, k, beta_c, gc_c)
    sol = each(lambda r, e_: r + _mm(e_, r), rhs, e)
    qk = each(lambda q_, t, dc: _mm_nt(q_, t) * dc, q, kb, decay)
    qg = each(lambda q_, gc: q_ * jnp.exp(gc), q, gc_c)

    state = [state_ref[hh] for hh in hr]
    v_new = [[] for _ in hr]
    cross = [[] for _ in hr]
    for c in range(rows // chunk):
        r0 = c * chunk
        rs = slice(r0, r0 + chunk)
        both = each(lambda s_, g_, st: _mm(jnp.concatenate([s_[rs, d:], g_[rs]], axis=0), st), sol, qg, state)
        vn = each(lambda s_, bt: s_[rs, :d] - bt[:chunk], sol, both)
        g_last = [gc[r0 + chunk - 1:r0 + chunk, :] for gc in gc_c]
        kd_t = each(lambda k_, gl, gc: jnp.transpose(k_[rs] * jnp.exp(gl - gc[rs])), k, g_last, gc_c)
        state = each(lambda st, gl, kt, vn_: st * jnp.exp(gl) + _mm(kt, vn_), state, g_last, kd_t, vn)
        for hh in hr:
            v_new[hh].append(vn[hh])
            cross[hh].append(both[hh][chunk:])
    for hh in hr:
        hs = slice(hh * d, (hh + 1) * d)
        state_ref[hh] = state[hh]
        o = jnp.concatenate(cross[hh], axis=0) + _mm(qk[hh], jnp.concatenate(v_new[hh], axis=0))
        o = o * lax.rsqrt(jnp.mean(o * o, axis=-1, keepdims=True) + NORM_EPS) * nw_ref[...]
        o_ref[:, hs] = (o * _silu(z_ref[:, hs].astype(F32))).astype(o_ref.dtype)


def _gdn(proj, ba, conv_w, a_log, dt_bias, norm_w, b, s):
    rows, heads = GDN_ROWS, GDN_HEADS_PER_STEP
    nt = s // rows
    hw = heads * HEAD_DIM
    groups = GDN_HEADS // heads
    col0 = B_OFF // hw
    gcol = GDN_WIDTH // hw
    row = lambda bi, hg, n: bi * nt + n
    smem = pl.BlockSpec(memory_space=pltpu.SMEM)

    def xspec(i):
        return pl.BlockSpec((rows, hw), lambda bi, hg, n: (row(bi, hg, n), col0 + i * gcol + hg))

    def cwspec(i):
        return pl.BlockSpec((GDN_CONV, hw), lambda bi, hg, n: (0, i * gcol + hg))

    return pl.pallas_call(
        functools.partial(_gdn_kernel, heads=heads),
        grid=(b, groups, nt),
        in_specs=[smem, smem, xspec(0), xspec(1), xspec(2), xspec(3), cwspec(0), cwspec(1), cwspec(2),
                  pl.BlockSpec((rows, BA_WIDTH), lambda bi, hg, n: (row(bi, hg, n), 0)),
                  pl.BlockSpec((2 * GDN_HEADS, rows), lambda bi, hg, n: (0, row(bi, hg, n))),
                  pl.BlockSpec((1, HEAD_DIM), lambda bi, hg, n: (0, 0))],
        out_specs=pl.BlockSpec((rows, hw), lambda bi, hg, n: (row(bi, hg, n), hg)),
        out_shape=jax.ShapeDtypeStruct((b * s, GDN_WIDTH), BF16),
        scratch_shapes=[pltpu.VMEM((rows + 8, 3 * hw), F32),
                        pltpu.VMEM((heads, HEAD_DIM, HEAD_DIM), F32)],
        compiler_params=_cparams("parallel", "parallel", "arbitrary"),
        name="gdn",
    )(a_log, dt_bias, proj, proj, proj, proj, conv_w, conv_w, conv_w, ba,
      jnp.transpose(ba[:, :2 * GDN_HEADS]), norm_w.reshape(1, HEAD_DIM))


def kernel(x, ln_mix_w, ln_ffn_w, w_in, gdn_conv_w, gdn_a_log, gdn_dt_bias, gdn_norm_w, attn_sinks,
           w_branch, w_out, w_up, ffn_conv_w, ffn_conv_b, w_down, ln_final_w):
    b, s, d = x.shape
    t = b * s
    depth = w_in.shape[0]
    tm = min(ROW_TILE, s)
    xf = x.reshape(t, d)
    cos_t, sin_t = _rope_tables(s)
    w_in_bf = w_in.astype(BF16)
    for layer in range(depth):
        conv_w_p = _interleave_gate_up(ffn_conv_w[layer])
        conv_b_p = _interleave_gate_up(ffn_conv_b[layer]).reshape(1, -1)

        h = _rmsnorm(xf, ln_mix_w[layer], BF16)
        w_in_p = _cast_pack_in(w_in_bf, layer, min(2048, d))
        proj = _matmul(h, w_in_p, BF16, tm, 1024 if w_in_p.shape[1] % 1024 == 0 else 512, "in_proj")
        ba = _gate_proj(h, w_in_bf, layer, tm)
        out_a = _swa(proj, attn_sinks[layer], cos_t, sin_t, b, s)
        out_b = _gdn(proj, ba, gdn_conv_w[layer], gdn_a_log[layer], gdn_dt_bias[layer], gdn_norm_w[layer], b, s)
        out_c = _retention(proj, cos_t, sin_t, b, s)
        merged = _branch_merge(out_a, out_b, out_c, _cast_bf16(w_branch, layer, 256), proj, d, tm, 512)
        xf = _matmul_residual(merged, _cast_bf16(w_out, layer, 256), xf, tm, 512, 1, "out_proj")

        h = _rmsnorm(xf, ln_ffn_w[layer], BF16)
        act = _up_glu(h, w_up, layer, conv_w_p, conv_b_p, s, tm)
        xf = _matmul_residual(act, _cast_bf16(w_down, layer, 256), xf, min(tm, 512), 512, 1, "down_proj")
    out = _rmsnorm(xf, ln_final_w, x.dtype)
    return out.reshape(b, s, d)
```

```python
import functools

import jax
import jax.numpy as jnp
from jax import lax
from jax.experimental import pallas as pl
from jax.experimental.pallas import tpu as pltpu

HEAD_DIM = 128
NORM_EPS = 1e-6
ROPE_THETA = 10000.0
SWA_Q_HEADS = 16
SWA_KV_HEADS = 4
SWA_GROUP = SWA_Q_HEADS // SWA_KV_HEADS
SWA_BLOCK = 128
GDN_HEADS = 8
GDN_CONV = 4
GDN_CHUNK = 64
GDN_ROWS = 256
GDN_HEADS_PER_STEP = 4
RET_HEADS = 8
RET_CHUNK = 128
FFN_CONV = 3
N_BRANCHES = 3

SWA_WIDTH = SWA_Q_HEADS * HEAD_DIM
SWA_KV_WIDTH = SWA_KV_HEADS * HEAD_DIM
GDN_WIDTH = GDN_HEADS * HEAD_DIM
RET_WIDTH = RET_HEADS * HEAD_DIM

A_OFF = 0
B_OFF = SWA_WIDTH + 2 * SWA_KV_WIDTH
C_OFF = B_OFF + 4 * GDN_WIDTH
G_OFF = C_OFF + 4 * RET_WIDTH
BA_WIDTH = 128

VMEM_LIMIT_BYTES = 52 * 1024 * 1024
FFN_TILE = 256
ROW_TILE = 1024
MASKED = -1e30

F32 = jnp.float32
BF16 = jnp.bfloat16
NT_DIMS = (((1,), (1,)), ((), ()))


def _cparams(*sem):
    return pltpu.CompilerParams(dimension_semantics=sem, vmem_limit_bytes=VMEM_LIMIT_BYTES)


def _mm(a, b):
    return jnp.dot(a.astype(BF16), b.astype(BF16), preferred_element_type=F32)


def _mm_nt(a, b):
    return lax.dot_general(a.astype(BF16), b.astype(BF16), NT_DIMS, preferred_element_type=F32)


def _silu(x):
    return x * jax.nn.sigmoid(x)


def _rope(x, cos, sin_signed):
    return x * cos + pltpu.roll(x, HEAD_DIM // 2, 1) * sin_signed


def _rmsnorm_kernel(x_ref, w_ref, o_ref):
    x = x_ref[...]
    y = x * lax.rsqrt(jnp.mean(x * x, axis=-1, keepdims=True) + NORM_EPS)
    o_ref[...] = (y * w_ref[...]).astype(o_ref.dtype)


def _rmsnorm(x, w, out_dtype, tm=256):
    t, d = x.shape
    return pl.pallas_call(
        _rmsnorm_kernel,
        grid=(t // tm,),
        in_specs=[pl.BlockSpec((tm, d), lambda i: (i, 0)),
                  pl.BlockSpec((1, d), lambda i: (0, 0))],
        out_specs=pl.BlockSpec((tm, d), lambda i: (i, 0)),
        out_shape=jax.ShapeDtypeStruct((t, d), out_dtype),
        compiler_params=_cparams("parallel"),
        name="rmsnorm",
    )(x, w.reshape(1, d))


def _gate_proj_kernel(x_ref, w_ref, o_ref):
    o_ref[...] = jnp.dot(x_ref[...], w_ref[...].astype(BF16), preferred_element_type=F32)


def _gate_proj(x, w_in, layer, tm):
    m, k = x.shape
    col_blk = (B_OFF + 4 * GDN_WIDTH) // BA_WIDTH
    return pl.pallas_call(
        _gate_proj_kernel,
        grid=(m // tm,),
        in_specs=[pl.BlockSpec((tm, k), lambda i: (i, 0)),
                  pl.BlockSpec((None, k, BA_WIDTH), lambda i: (layer, 0, col_blk))],
        out_specs=pl.BlockSpec((tm, BA_WIDTH), lambda i: (i, 0)),
        out_shape=jax.ShapeDtypeStruct((m, BA_WIDTH), F32),
        compiler_params=_cparams("parallel"),
        name="in_proj_gates",
    )(x, w_in)


def _mm_res_kernel(x_ref, w_ref, r_ref, o_ref, acc_ref, *, nk):
    kk = pl.program_id(2)
    part = jnp.dot(x_ref[...], w_ref[...].astype(BF16), preferred_element_type=F32)
    if nk == 1:
        o_ref[...] = r_ref[...] + part
    else:
        @pl.when(kk == 0)
        def _():
            acc_ref[...] = part

        @pl.when(jnp.logical_and(kk > 0, kk < nk - 1))
        def _():
            acc_ref[...] += part

        @pl.when(kk == nk - 1)
        def _():
            o_ref[...] = r_ref[...] + (acc_ref[...] + part)


def _matmul_residual(x, w, res, tm, tn, nk, name, layer=None):
    m, k = x.shape
    n = w.shape[-1]
    tk = k // nk
    if layer is None:
        w_spec = pl.BlockSpec((tk, tn), lambda i, j, kk: (kk, j))
    else:
        w_spec = pl.BlockSpec((None, tk, tn), lambda i, j, kk: (layer, kk, j))
    return pl.pallas_call(
        functools.partial(_mm_res_kernel, nk=nk),
        grid=(m // tm, n // tn, nk),
        in_specs=[pl.BlockSpec((tm, tk), lambda i, j, kk: (i, kk)),
                  w_spec,
                  pl.BlockSpec((tm, tn), lambda i, j, kk: (i, j))],
        out_specs=pl.BlockSpec((tm, tn), lambda i, j, kk: (i, j)),
        out_shape=jax.ShapeDtypeStruct((m, n), F32),
        scratch_shapes=[pltpu.VMEM((tm, tn), F32)],
        compiler_params=_cparams("parallel", "arbitrary", "arbitrary"),
        name=name,
    )(x, w, res)


def _merge_kernel(a_ref, b_ref, c_ref, wa_ref, wb_ref, wc_ref, ga_ref, gb_ref, gc_ref, o_ref):
    def branch(x_ref, w_ref, g_ref):
        p = jnp.dot(x_ref[...], w_ref[...].astype(BF16), preferred_element_type=F32)
        return jax.nn.sigmoid(g_ref[...].astype(F32)) * p

    merged = branch(a_ref, wa_ref, ga_ref) + branch(b_ref, wb_ref, gb_ref) + branch(c_ref, wc_ref, gc_ref)
    o_ref[...] = merged.astype(o_ref.dtype)


def _branch_merge(out_a, out_b, out_c, w_branch, layer, proj, d_model, tm, tn):
    t = out_a.shape[0]
    gate_blk = G_OFF // tn
    d_blk = d_model // tn
    b_row = SWA_WIDTH // GDN_WIDTH
    return pl.pallas_call(
        _merge_kernel,
        grid=(t // tm, d_model // tn),
        in_specs=[pl.BlockSpec((tm, SWA_WIDTH), lambda i, j: (i, 0)),
                  pl.BlockSpec((tm, GDN_WIDTH), lambda i, j: (i, 0)),
                  pl.BlockSpec((tm, RET_WIDTH), lambda i, j: (i, 0)),
                  pl.BlockSpec((None, SWA_WIDTH, tn), lambda i, j: (layer, 0, j)),
                  pl.BlockSpec((None, GDN_WIDTH, tn), lambda i, j: (layer, b_row, j)),
                  pl.BlockSpec((None, RET_WIDTH, tn), lambda i, j: (layer, b_row + 1, j)),
                  pl.BlockSpec((tm, tn), lambda i, j: (i, gate_blk + j)),
                  pl.BlockSpec((tm, tn), lambda i, j: (i, gate_blk + d_blk + j)),
                  pl.BlockSpec((tm, tn), lambda i, j: (i, gate_blk + 2 * d_blk + j))],
        out_specs=pl.BlockSpec((tm, tn), lambda i, j: (i, j)),
        out_shape=jax.ShapeDtypeStruct((t, d_model), BF16),
        compiler_params=_cparams("parallel", "arbitrary"),
        name="branch_merge",
    )(out_a, out_b, out_c, w_branch, w_branch, w_branch, proj, proj, proj)


def _upglu_kernel(x_ref, wg_ref, wu_ref, cw_ref, cb_ref, o_ref, u_ref, w_ref, *, tm, tiles_per_seq):
    mi = pl.program_id(1)

    @pl.when(mi == 0)
    def _():
        w_ref[:, :FFN_TILE] = wg_ref[...].astype(BF16)
        w_ref[:, FFN_TILE:] = wu_ref[...].astype(BF16)

    @pl.when(mi % tiles_per_seq == 0)
    def _():
        u_ref[0:8, :] = jnp.zeros((8, u_ref.shape[1]), F32)

    u_ref[8:8 + tm, :] = jnp.dot(x_ref[...], w_ref[...], preferred_element_type=F32)
    cw = cw_ref[...]
    y = cb_ref[...] + cw[2:3, :] * u_ref[8:8 + tm, :]
    y = y + cw[1:2, :] * u_ref[7:7 + tm, :]
    y = y + cw[0:1, :] * u_ref[6:6 + tm, :]
    u_ref[0:8, :] = u_ref[tm:tm + 8, :]
    o_ref[...] = (_silu(y[:, :FFN_TILE]) * y[:, FFN_TILE:]).astype(o_ref.dtype)


def _up_glu(h, w_up, layer, conv_w_p, conv_b_p, seq, tm):
    t, d = h.shape
    n2 = w_up.shape[2]
    nblk = n2 // (2 * FFN_TILE)
    return pl.pallas_call(
        functools.partial(_upglu_kernel, tm=tm, tiles_per_seq=seq // tm),
        grid=(nblk, t // tm),
        in_specs=[pl.BlockSpec((tm, d), lambda j, i: (i, 0)),
                  pl.BlockSpec((None, d, FFN_TILE), lambda j, i: (layer, 0, j)),
                  pl.BlockSpec((None, d, FFN_TILE), lambda j, i: (layer, 0, nblk + j)),
                  pl.BlockSpec((FFN_CONV, 2 * FFN_TILE), lambda j, i: (0, j)),
                  pl.BlockSpec((1, 2 * FFN_TILE), lambda j, i: (0, j))],
        out_specs=pl.BlockSpec((tm, FFN_TILE), lambda j, i: (i, j)),
        out_shape=jax.ShapeDtypeStruct((t, nblk * FFN_TILE), BF16),
        scratch_shapes=[pltpu.VMEM((tm + 8, 2 * FFN_TILE), F32),
                        pltpu.VMEM((d, 2 * FFN_TILE), BF16)],
        compiler_params=_cparams("parallel", "arbitrary"),
        name="up_conv_glu",
    )(h, w_up, w_up, conv_w_p, conv_b_p)


def _cast_kernel(w_ref, o_ref):
    o_ref[...] = w_ref[...].astype(o_ref.dtype)


def _cast_bf16(w, layer, tk):
    _, k, n = w.shape
    return pl.pallas_call(
        _cast_kernel,
        grid=(k // tk,),
        in_specs=[pl.BlockSpec((None, tk, n), lambda i: (layer, i, 0))],
        out_specs=pl.BlockSpec((tk, n), lambda i: (i, 0)),
        out_shape=jax.ShapeDtypeStruct((k, n), BF16),
        compiler_params=_cparams("parallel"),
        name="cast_bf16",
    )(w)


IN_PACK_SHIFT = 2 * GDN_HEADS


def _inproj_kernel(x_ref, a_ref, b_ref, o_ref, *, first_shifted):
    j = pl.program_id(1)

    @pl.when(j < first_shifted)
    def _():
        o_ref[...] = jnp.dot(x_ref[...], a_ref[...], preferred_element_type=F32).astype(o_ref.dtype)

    @pl.when(j >= first_shifted)
    def _():
        w = jnp.concatenate([a_ref[:, IN_PACK_SHIFT:], b_ref[:, :IN_PACK_SHIFT]], axis=1)
        o_ref[...] = jnp.dot(x_ref[...], w, preferred_element_type=F32).astype(o_ref.dtype)


def _in_proj(x, w_in_bf, layer, tm, tn):
    m, k = x.shape
    n_packed = w_in_bf.shape[2] - IN_PACK_SHIFT
    ba_lo = B_OFF + 4 * GDN_WIDTH
    lanes = HEAD_DIM
    return pl.pallas_call(
        functools.partial(_inproj_kernel, first_shifted=ba_lo // tn),
        grid=(m // tm, n_packed // tn),
        in_specs=[pl.BlockSpec((tm, k), lambda i, j: (i, 0)),
                  pl.BlockSpec((None, k, tn), lambda i, j: (layer, 0, j)),
                  pl.BlockSpec((None, k, lanes), lambda i, j: (layer, 0, (tn // lanes) * (j + 1)))],
        out_specs=pl.BlockSpec((tm, tn), lambda i, j: (i, j)),
        out_shape=jax.ShapeDtypeStruct((m, n_packed), BF16),
        compiler_params=_cparams("parallel", "arbitrary"),
        name="in_proj",
    )(x, w_in_bf, w_in_bf)


def _interleave_gate_up(a):
    lead = a.shape[:-1]
    f = a.shape[-1] // 2
    a = a.reshape(lead + (2, f // FFN_TILE, FFN_TILE))
    a = jnp.swapaxes(a, -3, -2)
    return a.reshape(lead + (2 * f,))


def _rope_tables(s):
    half = HEAD_DIM // 2
    inv_freq = ROPE_THETA ** (-jnp.arange(half, dtype=F32) / half)
    ang = jnp.arange(s, dtype=F32)[:, None] * inv_freq[None, :]
    cos, sin = jnp.cos(ang), jnp.sin(ang)
    return jnp.concatenate([cos, cos], axis=1), jnp.concatenate([-sin, sin], axis=1)


def _swa_kernel(sink_ref, q_ref, kc_ref, kp_ref, vc_ref, vp_ref, cos_ref, sin_ref, cosp_ref, sinp_ref, o_ref):
    n = pl.program_id(1)
    blk = SWA_BLOCK
    cos, sin = cos_ref[...], sin_ref[...]
    cosp, sinp = cosp_ref[...], sinp_ref[...]
    qi = lax.broadcasted_iota(jnp.int32, (blk, 2 * blk), 0)
    kj = lax.broadcasted_iota(jnp.int32, (blk, 2 * blk), 1)
    no_prev = jnp.where(n > 0, 0, blk)
    valid = jnp.where(kj < blk, kj - qi - no_prev, qi - kj + blk + 1) > 0
    scale = HEAD_DIM ** -0.5
    for hk in range(SWA_KV_HEADS):
        sl = slice(hk * HEAD_DIM, (hk + 1) * HEAD_DIM)
        kc = _rope(kc_ref[:, sl].astype(F32), cos, sin)
        kp = _rope(kp_ref[:, sl].astype(F32), cosp, sinp)
        k2 = jnp.concatenate([kp, kc], axis=0).astype(BF16)
        v2 = jnp.concatenate([vp_ref[:, sl], vc_ref[:, sl]], axis=0)
        qs = []
        for g in range(SWA_GROUP):
            hq = hk * SWA_GROUP + g
            qh = _rope(q_ref[:, hq * HEAD_DIM:(hq + 1) * HEAD_DIM].astype(F32), cos, sin) * scale
            qs.append(qh.astype(BF16))
        scores = _mm_nt(jnp.concatenate(qs, axis=0), k2)
        probs, inv_den = [], []
        for g in range(SWA_GROUP):
            sink = sink_ref[hk * SWA_GROUP + g]
            sg = jnp.where(valid, scores[g * blk:(g + 1) * blk], MASKED)
            m = jnp.maximum(jnp.max(sg, axis=1, keepdims=True), sink)
            p = jnp.exp(sg - m)
            den = jnp.sum(p, axis=1, keepdims=True) + jnp.exp(sink - m)
            probs.append(p.astype(BF16))
            inv_den.append(1.0 / den)
        out = jnp.dot(jnp.concatenate(probs, axis=0), v2, preferred_element_type=F32)
        for g in range(SWA_GROUP):
            hq = hk * SWA_GROUP + g
            o_ref[:, hq * HEAD_DIM:(hq + 1) * HEAD_DIM] = (out[g * blk:(g + 1) * blk] * inv_den[g]).astype(o_ref.dtype)


def _swa(proj, sinks, cos_t, sin_t, b, s):
    nb = s // SWA_BLOCK
    kcol = (A_OFF + SWA_WIDTH) // SWA_KV_WIDTH
    vcol = kcol + 1
    cur = lambda bi, n: (bi * nb + n, 0)
    prev_rows = lambda bi, n: bi * nb + jnp.maximum(n - 1, 0)
    return pl.pallas_call(
        _swa_kernel,
        grid=(b, nb),
        in_specs=[pl.BlockSpec(memory_space=pltpu.SMEM),
                  pl.BlockSpec((SWA_BLOCK, SWA_WIDTH), cur),
                  pl.BlockSpec((SWA_BLOCK, SWA_KV_WIDTH), lambda bi, n: (bi * nb + n, kcol)),
                  pl.BlockSpec((SWA_BLOCK, SWA_KV_WIDTH), lambda bi, n: (prev_rows(bi, n), kcol)),
                  pl.BlockSpec((SWA_BLOCK, SWA_KV_WIDTH), lambda bi, n: (bi * nb + n, vcol)),
                  pl.BlockSpec((SWA_BLOCK, SWA_KV_WIDTH), lambda bi, n: (prev_rows(bi, n), vcol)),
                  pl.BlockSpec((SWA_BLOCK, HEAD_DIM), lambda bi, n: (n, 0)),
                  pl.BlockSpec((SWA_BLOCK, HEAD_DIM), lambda bi, n: (n, 0)),
                  pl.BlockSpec((SWA_BLOCK, HEAD_DIM), lambda bi, n: (jnp.maximum(n - 1, 0), 0)),
                  pl.BlockSpec((SWA_BLOCK, HEAD_DIM), lambda bi, n: (jnp.maximum(n - 1, 0), 0))],
        out_specs=pl.BlockSpec((SWA_BLOCK, SWA_WIDTH), cur),
        out_shape=jax.ShapeDtypeStruct((b * s, SWA_WIDTH), BF16),
        compiler_params=_cparams("parallel", "parallel"),
        name="swa",
    )(sinks, proj, proj, proj, proj, proj, cos_t, sin_t, cos_t, sin_t)


def _ret_kernel(q_ref, k_ref, v_ref, g_ref, cos_ref, sin_ref, dm_ref, xi_ref, zeta_ref, gch_ref, o_ref, state_ref):
    @pl.when(pl.program_id(1) == 0)
    def _():
        state_ref[...] = jnp.zeros(state_ref.shape, F32)

    cos, sin = cos_ref[...], sin_ref[...]
    for h in range(RET_HEADS):
        sl = slice(h * HEAD_DIM, (h + 1) * HEAD_DIM)
        q = _rope(q_ref[:, sl].astype(F32), cos, sin)
        k = _rope(k_ref[:, sl].astype(F32), cos, sin) * (HEAD_DIM ** -0.5)
        v = v_ref[:, sl]
        scores = _mm_nt(q, k) * dm_ref[h]
        inner = _mm(scores, v)
        state = state_ref[h]
        cross = _mm(q, state) * xi_ref[h]
        kz_t = jnp.transpose(k * zeta_ref[h])
        state_ref[h] = state * gch_ref[h] + _mm(kz_t, v)
        o = inner + cross
        mu = jnp.mean(o, axis=-1, keepdims=True)
        var = jnp.mean(jnp.square(o - mu), axis=-1, keepdims=True)
        o = (o - mu) * lax.rsqrt(var + NORM_EPS)
        o_ref[:, sl] = (_silu(g_ref[:, sl].astype(F32)) * o).astype(o_ref.dtype)


def _retention(proj, cos_t, sin_t, b, s):
    c = RET_CHUNK
    nc = s // c
    h = RET_HEADS
    log_gamma = jnp.log(1.0 - 2.0 ** (-5.0 - jnp.arange(h, dtype=F32)))
    idx = jnp.arange(c, dtype=F32)
    rel = idx[:, None] - idx[None, :]
    d_mask = jnp.where(rel >= 0, jnp.exp(jnp.maximum(rel, 0.0)[None] * log_gamma[:, None, None]), 0.0)
    xi = jnp.exp((idx + 1.0)[None, :] * log_gamma[:, None])
    zeta = jnp.exp((c - 1.0 - idx)[None, :] * log_gamma[:, None])
    gamma_chunk = jnp.exp(c * log_gamma)
    xi_b = jnp.broadcast_to(xi[:, :, None], (h, c, HEAD_DIM))
    zeta_b = jnp.broadcast_to(zeta[:, :, None], (h, c, HEAD_DIM))
    gch_b = jnp.broadcast_to(gamma_chunk[:, None, None], (h, 1, HEAD_DIM))
    col0 = C_OFF // RET_WIDTH
    whole3 = lambda bi, n: (0, 0, 0)
    row = lambda bi, n: bi * nc + n
    return pl.pallas_call(
        _ret_kernel,
        grid=(b, nc),
        in_specs=[pl.BlockSpec((c, RET_WIDTH), lambda bi, n: (row(bi, n), col0)),
                  pl.BlockSpec((c, RET_WIDTH), lambda bi, n: (row(bi, n), col0 + 1)),
                  pl.BlockSpec((c, RET_WIDTH), lambda bi, n: (row(bi, n), col0 + 2)),
                  pl.BlockSpec((c, RET_WIDTH), lambda bi, n: (row(bi, n), col0 + 3)),
                  pl.BlockSpec((c, HEAD_DIM), lambda bi, n: (n, 0)),
                  pl.BlockSpec((c, HEAD_DIM), lambda bi, n: (n, 0)),
                  pl.BlockSpec((h, c, c), whole3),
                  pl.BlockSpec((h, c, HEAD_DIM), whole3),
                  pl.BlockSpec((h, c, HEAD_DIM), whole3),
                  pl.BlockSpec((h, 1, HEAD_DIM), whole3)],
        out_specs=pl.BlockSpec((c, RET_WIDTH), lambda bi, n: (row(bi, n), 0)),
        out_shape=jax.ShapeDtypeStruct((b * s, RET_WIDTH), BF16),
        scratch_shapes=[pltpu.VMEM((h, HEAD_DIM, HEAD_DIM), F32)],
        compiler_params=_cparams("parallel", "arbitrary"),
        name="retention",
    )(proj, proj, proj, proj, cos_t, sin_t, d_mask, xi_b, zeta_b, gch_b)


def _softplus(x):
    return jnp.maximum(x, 0.0) + jnp.log(1.0 + jnp.exp(-jnp.abs(x)))


def _gdn_kernel(alog_ref, dtb_ref, q_ref, k_ref, v_ref, z_ref, cwq_ref, cwk_ref, cwv_ref, ba_ref, bat_ref, nw_ref,
                o_ref, cbuf_ref, state_ref, *, heads):
    hg = pl.program_id(1)
    n = pl.program_id(2)
    rows, chunk, d = GDN_ROWS, GDN_CHUNK, HEAD_DIM
    hw = heads * d

    @pl.when(n == 0)
    def _():
        state_ref[...] = jnp.zeros(state_ref.shape, F32)
        cbuf_ref[0:8, :] = jnp.zeros((8, cbuf_ref.shape[1]), F32)

    for i, (x_ref, cw_ref) in enumerate(((q_ref, cwq_ref), (k_ref, cwk_ref), (v_ref, cwv_ref))):
        cs = slice(i * hw, (i + 1) * hw)
        cbuf_ref[8:8 + rows, cs] = x_ref[...].astype(F32)
        cw = cw_ref[...]
        y = cw[3:4, :] * cbuf_ref[8:8 + rows, cs]
        for tap in range(1, GDN_CONV):
            y = y + cw[3 - tap:4 - tap, :] * cbuf_ref[8 - tap:8 - tap + rows, cs]
        cbuf_ref[0:8, cs] = cbuf_ref[rows:rows + 8, cs]
        cbuf_ref[8:8 + rows, cs] = _silu(y)

    ri = lax.broadcasted_iota(jnp.int32, (rows, rows), 0)
    ci = lax.broadcasted_iota(jnp.int32, (rows, rows), 1)

    def same(blk):
        sh = blk.bit_length() - 1
        return jnp.right_shift(ri, sh) == jnp.right_shift(ci, sh)

    causal = jnp.logical_and(same(chunk), ci <= ri)
    causal_t = jnp.logical_and(same(chunk), ri <= ci)
    strict = jnp.logical_and(same(chunk), ci < ri)
    lane = lax.broadcasted_iota(jnp.int32, (rows, BA_WIDTH), 1)
    ba = ba_ref[...]

    hr = range(heads)
    each = lambda f, *ls: [f(*a) for a in zip(*ls)]
    q = [cbuf_ref[8:8 + rows, hh * d:(hh + 1) * d] for hh in hr]
    k = [cbuf_ref[8:8 + rows, hw + hh * d:hw + (hh + 1) * d] for hh in hr]
    v = [cbuf_ref[8:8 + rows, 2 * hw + hh * d:2 * hw + (hh + 1) * d] for hh in hr]
    q = each(lambda t: t * lax.rsqrt(jnp.sum(t * t, axis=-1, keepdims=True) + NORM_EPS) * (d ** -0.5), q)
    k = each(lambda t: t * lax.rsqrt(jnp.sum(t * t, axis=-1, keepdims=True) + NORM_EPS), k)
    kb = each(lambda t: t.astype(BF16), k)

    head = [hg * heads + hh for hh in hr]
    a_neg = [-jnp.exp(alog_ref[h]) for h in head]
    dtb = [dtb_ref[h] for h in head]
    beta_c = [jax.nn.sigmoid(jnp.sum(jnp.where(lane == h, ba, 0.0), axis=1, keepdims=True)) for h in head]
    alpha_c = [jnp.sum(jnp.where(lane == GDN_HEADS + h, ba, 0.0), axis=1, keepdims=True) for h in head]
    alpha_r = [bat_ref[pl.ds(GDN_HEADS + h, 1), :] for h in head]
    g_c = each(lambda a, x, t: a * _softplus(x + t), a_neg, alpha_c, dtb)
    g_r = each(lambda a, x, t: a * _softplus(x + t), a_neg, alpha_r, dtb)
    gc_c = each(lambda g: jnp.sum(jnp.where(causal, g, 0.0), axis=1, keepdims=True), g_r)
    gc_r = each(lambda g: jnp.sum(jnp.where(causal_t, g, 0.0), axis=0, keepdims=True), g_c)
    decay = each(lambda c, r: jnp.exp(jnp.where(causal, c - r, MASKED)), gc_c, gc_r)

    low = each(lambda bc, t, dc: jnp.where(strict, bc * _mm_nt(t, t) * dc, 0.0), beta_c, kb, decay)
    dg = each(lambda t: jnp.where(same(16), t, 0.0), low)
    d2 = each(lambda t: _mm(t, t), dg)
    d4 = each(lambda t: _mm(t, t), d2)
    e1 = each(lambda a, b2: b2 - a - _mm(a, b2), dg, d2)
    d8 = each(lambda t: _mm(t, t), d4)
    e2 = each(lambda a, b2: a + b2 + _mm(a, b2), d4, d8)
    e = each(lambda a, b2: a + b2 + _mm(a, b2), e1, e2)
    for blk in (16, 32):
        off_mask = jnp.logical_and(same(2 * blk), jnp.logical_not(same(blk)))
        off = each(lambda t: jnp.where(off_mask, t, 0.0), low)
        y = each(lambda o_, e_: o_ + _mm(o_, e_), off, e)
        e = each(lambda e_, y_: e_ - y_ - _mm(e_, y_), e, y)

    rhs = each(lambda v_, k_, bc, gc: jnp.concatenate([v_ * bc, k_ * (bc * jnp.exp(gc))], axis=1), v, k, beta_c, gc_c)
    sol = each(lambda r, e_: r + _mm(e_, r), rhs, e)
    qk = each(lambda q_, t, dc: _mm_nt(q_, t) * dc, q, kb, decay)
    qg = each(lambda q_, gc: q_ * jnp.exp(gc), q, gc_c)

    state = [state_ref[hh] for hh in hr]
    v_new = [[] for _ in hr]
    cross = [[] for _ in hr]
    for c in range(rows // chunk):
        r0 = c * chunk
        rs = slice(r0, r0 + chunk)
        both = each(lambda s_, g_, st: _mm(jnp.concatenate([s_[rs, d:], g_[rs]], axis=0), st), sol, qg, state)
        vn = each(lambda s_, bt: s_[rs, :d] - bt[:chunk], sol, both)
        g_last = [gc[r0 + chunk - 1:r0 + chunk, :] for gc in gc_c]
        kd_t = each(lambda k_, gl, gc: jnp.transpose(k_[rs] * jnp.exp(gl - gc[rs])), k, g_last, gc_c)
        state = each(lambda st, gl, kt, vn_: st * jnp.exp(gl) + _mm(kt, vn_), state, g_last, kd_t, vn)
        for hh in hr:
            v_new[hh].append(vn[hh])
            cross[hh].append(both[hh][chunk:])
    for hh in hr:
        hs = slice(hh * d, (hh + 1) * d)
        state_ref[hh] = state[hh]
        o = jnp.concatenate(cross[hh], axis=0) + _mm(qk[hh], jnp.concatenate(v_new[hh], axis=0))
        o = o * lax.rsqrt(jnp.mean(o * o, axis=-1, keepdims=True) + NORM_EPS) * nw_ref[...]
        o_ref[:, hs] = (o * _silu(z_ref[:, hs].astype(F32))).astype(o_ref.dtype)


def _gdn(proj, ba, conv_w, a_log, dt_bias, norm_w, b, s):
    rows, heads = GDN_ROWS, GDN_HEADS_PER_STEP
    nt = s // rows
    hw = heads * HEAD_DIM
    groups = GDN_HEADS // heads
    col0 = B_OFF // hw
    gcol = GDN_WIDTH // hw
    row = lambda bi, hg, n: bi * nt + n
    smem = pl.BlockSpec(memory_space=pltpu.SMEM)

    def xspec(i):
        return pl.BlockSpec((rows, hw), lambda bi, hg, n: (row(bi, hg, n), col0 + i * gcol + hg))

    def cwspec(i):
        return pl.BlockSpec((GDN_CONV, hw), lambda bi, hg, n: (0, i * gcol + hg))

    return pl.pallas_call(
        functools.partial(_gdn_kernel, heads=heads),
        grid=(b, groups, nt),
        in_specs=[smem, smem, xspec(0), xspec(1), xspec(2), xspec(3), cwspec(0), cwspec(1), cwspec(2),
                  pl.BlockSpec((rows, BA_WIDTH), lambda bi, hg, n: (row(bi, hg, n), 0)),
                  pl.BlockSpec((2 * GDN_HEADS, rows), lambda bi, hg, n: (0, row(bi, hg, n))),
                  pl.BlockSpec((1, HEAD_DIM), lambda bi, hg, n: (0, 0))],
        out_specs=pl.BlockSpec((rows, hw), lambda bi, hg, n: (row(bi, hg, n), hg)),
        out_shape=jax.ShapeDtypeStruct((b * s, GDN_WIDTH), BF16),
        scratch_shapes=[pltpu.VMEM((rows + 8, 3 * hw), F32),
                        pltpu.VMEM((heads, HEAD_DIM, HEAD_DIM), F32)],
        compiler_params=_cparams("parallel", "parallel", "arbitrary"),
        name="gdn",
    )(a_log, dt_bias, proj, proj, proj, proj, conv_w, conv_w, conv_w, ba,
      jnp.transpose(ba[:, :2 * GDN_HEADS]), norm_w.reshape(1, HEAD_DIM))


def kernel(x, ln_mix_w, ln_ffn_w, w_in, gdn_conv_w, gdn_a_log, gdn_dt_bias, gdn_norm_w, attn_sinks,
           w_branch, w_out, w_up, ffn_conv_w, ffn_conv_b, w_down, ln_final_w):
    b, s, d = x.shape
    t = b * s
    depth = w_in.shape[0]
    tm = min(ROW_TILE, s)
    xf = x.reshape(t, d)
    cos_t, sin_t = _rope_tables(s)
    w_in_bf = w_in.astype(BF16)
    for layer in range(depth):
        conv_w_p = _interleave_gate_up(ffn_conv_w[layer])
        conv_b_p = _interleave_gate_up(ffn_conv_b[layer]).reshape(1, -1)

        h = _rmsnorm(xf, ln_mix_w[layer], BF16)
        n_packed = w_in.shape[2] - IN_PACK_SHIFT
        proj = _in_proj(h, w_in_bf, layer, tm, 1024 if n_packed % 1024 == 0 else 512)
        ba = _gate_proj(h, w_in_bf, layer, tm)
        out_a = _swa(proj, attn_sinks[layer], cos_t, sin_t, b, s)
        out_b = _gdn(proj, ba, gdn_conv_w[layer], gdn_a_log[layer], gdn_dt_bias[layer], gdn_norm_w[layer], b, s)
        out_c = _retention(proj, cos_t, sin_t, b, s)
        merged = _branch_merge(out_a, out_b, out_c, w_branch, layer, proj, d, tm, 512)
        xf = _matmul_residual(merged, w_out, xf, tm, 512, 1, "out_proj", layer=layer)

        h = _rmsnorm(xf, ln_ffn_w[layer], BF16)
        act = _up_glu(h, w_up, layer, conv_w_p, conv_b_p, s, tm)
        xf = _matmul_residual(act, _cast_bf16(w_down, layer, 256), xf, min(tm, 512), 512, 1, "down_proj")
    out = _rmsnorm(xf, ln_final_w, x.dtype)
    return out.reshape(b, s, d)
```

```python
import functools

import jax
import jax.numpy as jnp
from jax import lax
from jax.experimental import pallas as pl
from jax.experimental.pallas import tpu as pltpu

HEAD_DIM = 128
NORM_EPS = 1e-6
ROPE_THETA = 10000.0
SWA_Q_HEADS = 16
SWA_KV_HEADS = 4
SWA_GROUP = SWA_Q_HEADS // SWA_KV_HEADS
SWA_BLOCK = 128
GDN_HEADS = 8
GDN_CONV = 4
GDN_CHUNK = 64
GDN_ROWS = 256
GDN_HEADS_PER_STEP = 8
RET_HEADS = 8
RET_CHUNK = 128
FFN_CONV = 3
N_BRANCHES = 3

SWA_WIDTH = SWA_Q_HEADS * HEAD_DIM
SWA_KV_WIDTH = SWA_KV_HEADS * HEAD_DIM
GDN_WIDTH = GDN_HEADS * HEAD_DIM
RET_WIDTH = RET_HEADS * HEAD_DIM

A_OFF = 0
B_OFF = SWA_WIDTH + 2 * SWA_KV_WIDTH
C_OFF = B_OFF + 4 * GDN_WIDTH
G_OFF = C_OFF + 4 * RET_WIDTH
BA_WIDTH = 128

VMEM_LIMIT_BYTES = 52 * 1024 * 1024
FFN_TILE = 256
ROW_TILE = 1024
MASKED = -1e30

F32 = jnp.float32
BF16 = jnp.bfloat16
NT_DIMS = (((1,), (1,)), ((), ()))


def _cparams(*sem):
    return pltpu.CompilerParams(dimension_semantics=sem, vmem_limit_bytes=VMEM_LIMIT_BYTES)


def _mm(a, b):
    return jnp.dot(a.astype(BF16), b.astype(BF16), preferred_element_type=F32)


def _mm_nt(a, b):
    return lax.dot_general(a.astype(BF16), b.astype(BF16), NT_DIMS, preferred_element_type=F32)


def _silu(x):
    return x * jax.nn.sigmoid(x)


def _rope(x, cos, sin_signed):
    return x * cos + pltpu.roll(x, HEAD_DIM // 2, 1) * sin_signed


def _rmsnorm_kernel(x_ref, w_ref, o_ref):
    x = x_ref[...]
    y = x * lax.rsqrt(jnp.mean(x * x, axis=-1, keepdims=True) + NORM_EPS)
    o_ref[...] = (y * w_ref[...]).astype(o_ref.dtype)


def _rmsnorm(x, w, out_dtype, tm=256):
    t, d = x.shape
    return pl.pallas_call(
        _rmsnorm_kernel,
        grid=(t // tm,),
        in_specs=[pl.BlockSpec((tm, d), lambda i: (i, 0)),
                  pl.BlockSpec((1, d), lambda i: (0, 0))],
        out_specs=pl.BlockSpec((tm, d), lambda i: (i, 0)),
        out_shape=jax.ShapeDtypeStruct((t, d), out_dtype),
        compiler_params=_cparams("parallel"),
        name="rmsnorm",
    )(x, w.reshape(1, d))


def _gate_proj_kernel(x_ref, w_ref, o_ref):
    o_ref[...] = jnp.dot(x_ref[...], w_ref[...].astype(BF16), preferred_element_type=F32)


def _gate_proj(x, w_in, layer, tm):
    m, k = x.shape
    col_blk = (B_OFF + 4 * GDN_WIDTH) // BA_WIDTH
    return pl.pallas_call(
        _gate_proj_kernel,
        grid=(m // tm,),
        in_specs=[pl.BlockSpec((tm, k), lambda i: (i, 0)),
                  pl.BlockSpec((None, k, BA_WIDTH), lambda i: (layer, 0, col_blk))],
        out_specs=pl.BlockSpec((tm, BA_WIDTH), lambda i: (i, 0)),
        out_shape=jax.ShapeDtypeStruct((m, BA_WIDTH), F32),
        compiler_params=_cparams("parallel"),
        name="in_proj_gates",
    )(x, w_in)


def _mm_res_kernel(x_ref, w_ref, r_ref, o_ref, acc_ref, *, nk):
    kk = pl.program_id(2)
    part = jnp.dot(x_ref[...], w_ref[...].astype(BF16), preferred_element_type=F32)
    if nk == 1:
        o_ref[...] = r_ref[...] + part
    else:
        @pl.when(kk == 0)
        def _():
            acc_ref[...] = part

        @pl.when(jnp.logical_and(kk > 0, kk < nk - 1))
        def _():
            acc_ref[...] += part

        @pl.when(kk == nk - 1)
        def _():
            o_ref[...] = r_ref[...] + (acc_ref[...] + part)


def _matmul_residual(x, w, res, tm, tn, nk, name, layer=None):
    m, k = x.shape
    n = w.shape[-1]
    tk = k // nk
    if layer is None:
        w_spec = pl.BlockSpec((tk, tn), lambda i, j, kk: (kk, j))
    else:
        w_spec = pl.BlockSpec((None, tk, tn), lambda i, j, kk: (layer, kk, j))
    return pl.pallas_call(
        functools.partial(_mm_res_kernel, nk=nk),
        grid=(m // tm, n // tn, nk),
        in_specs=[pl.BlockSpec((tm, tk), lambda i, j, kk: (i, kk)),
                  w_spec,
                  pl.BlockSpec((tm, tn), lambda i, j, kk: (i, j))],
        out_specs=pl.BlockSpec((tm, tn), lambda i, j, kk: (i, j)),
        out_shape=jax.ShapeDtypeStruct((m, n), F32),
        scratch_shapes=[pltpu.VMEM((tm, tn), F32)],
        compiler_params=_cparams("parallel", "arbitrary", "arbitrary"),
        name=name,
    )(x, w, res)


def _merge_kernel(a_ref, b_ref, c_ref, wa_ref, wb_ref, wc_ref, ga_ref, gb_ref, gc_ref, o_ref):
    def branch(x_ref, w_ref, g_ref):
        p = jnp.dot(x_ref[...], w_ref[...].astype(BF16), preferred_element_type=F32)
        return jax.nn.sigmoid(g_ref[...].astype(F32)) * p

    merged = branch(a_ref, wa_ref, ga_ref) + branch(b_ref, wb_ref, gb_ref) + branch(c_ref, wc_ref, gc_ref)
    o_ref[...] = merged.astype(o_ref.dtype)


def _branch_merge(out_a, out_b, out_c, w_branch, layer, proj, d_model, tm, tn):
    t = out_a.shape[0]
    gate_blk = G_OFF // tn
    d_blk = d_model // tn
    b_row = SWA_WIDTH // GDN_WIDTH
    return pl.pallas_call(
        _merge_kernel,
        grid=(t // tm, d_model // tn),
        in_specs=[pl.BlockSpec((tm, SWA_WIDTH), lambda i, j: (i, 0)),
                  pl.BlockSpec((tm, GDN_WIDTH), lambda i, j: (i, 0)),
                  pl.BlockSpec((tm, RET_WIDTH), lambda i, j: (i, 0)),
                  pl.BlockSpec((None, SWA_WIDTH, tn), lambda i, j: (layer, 0, j)),
                  pl.BlockSpec((None, GDN_WIDTH, tn), lambda i, j: (layer, b_row, j)),
                  pl.BlockSpec((None, RET_WIDTH, tn), lambda i, j: (layer, b_row + 1, j)),
                  pl.BlockSpec((tm, tn), lambda i, j: (i, gate_blk + j)),
                  pl.BlockSpec((tm, tn), lambda i, j: (i, gate_blk + d_blk + j)),
                  pl.BlockSpec((tm, tn), lambda i, j: (i, gate_blk + 2 * d_blk + j))],
        out_specs=pl.BlockSpec((tm, tn), lambda i, j: (i, j)),
        out_shape=jax.ShapeDtypeStruct((t, d_model), BF16),
        compiler_params=_cparams("parallel", "arbitrary"),
        name="branch_merge",
    )(out_a, out_b, out_c, w_branch, w_branch, w_branch, proj, proj, proj)


def _upglu_kernel(x_ref, wg_ref, wu_ref, cw_ref, cb_ref, o_ref, u_ref, w_ref, *, tm, tiles_per_seq):
    mi = pl.program_id(1)

    @pl.when(mi == 0)
    def _():
        w_ref[:, :FFN_TILE] = wg_ref[...].astype(BF16)
        w_ref[:, FFN_TILE:] = wu_ref[...].astype(BF16)

    @pl.when(mi % tiles_per_seq == 0)
    def _():
        u_ref[0:8, :] = jnp.zeros((8, u_ref.shape[1]), F32)

    u_ref[8:8 + tm, :] = jnp.dot(x_ref[...], w_ref[...], preferred_element_type=F32)
    cw = cw_ref[...]
    y = cb_ref[...] + cw[2:3, :] * u_ref[8:8 + tm, :]
    y = y + cw[1:2, :] * u_ref[7:7 + tm, :]
    y = y + cw[0:1, :] * u_ref[6:6 + tm, :]
    u_ref[0:8, :] = u_ref[tm:tm + 8, :]
    o_ref[...] = (_silu(y[:, :FFN_TILE]) * y[:, FFN_TILE:]).astype(o_ref.dtype)


def _up_glu(h, w_up, layer, conv_w_p, conv_b_p, seq, tm):
    t, d = h.shape
    n2 = w_up.shape[2]
    nblk = n2 // (2 * FFN_TILE)
    return pl.pallas_call(
        functools.partial(_upglu_kernel, tm=tm, tiles_per_seq=seq // tm),
        grid=(nblk, t // tm),
        in_specs=[pl.BlockSpec((tm, d), lambda j, i: (i, 0)),
                  pl.BlockSpec((None, d, FFN_TILE), lambda j, i: (layer, 0, j)),
                  pl.BlockSpec((None, d, FFN_TILE), lambda j, i: (layer, 0, nblk + j)),
                  pl.BlockSpec((FFN_CONV, 2 * FFN_TILE), lambda j, i: (0, j)),
                  pl.BlockSpec((1, 2 * FFN_TILE), lambda j, i: (0, j))],
        out_specs=pl.BlockSpec((tm, FFN_TILE), lambda j, i: (i, j)),
        out_shape=jax.ShapeDtypeStruct((t, nblk * FFN_TILE), BF16),
        scratch_shapes=[pltpu.VMEM((tm + 8, 2 * FFN_TILE), F32),
                        pltpu.VMEM((d, 2 * FFN_TILE), BF16)],
        compiler_params=_cparams("parallel", "arbitrary"),
        name="up_conv_glu",
    )(h, w_up, w_up, conv_w_p, conv_b_p)


def _cast_kernel(w_ref, o_ref):
    o_ref[...] = w_ref[...].astype(o_ref.dtype)


def _cast_bf16(w, layer, tk):
    _, k, n = w.shape
    return pl.pallas_call(
        _cast_kernel,
        grid=(k // tk,),
        in_specs=[pl.BlockSpec((None, tk, n), lambda i: (layer, i, 0))],
        out_specs=pl.BlockSpec((tk, n), lambda i: (i, 0)),
        out_shape=jax.ShapeDtypeStruct((k, n), BF16),
        compiler_params=_cparams("parallel"),
        name="cast_bf16",
    )(w)


IN_PACK_SHIFT = 2 * GDN_HEADS


def _inproj_kernel(x_ref, a_ref, b_ref, o_ref, *, first_shifted):
    j = pl.program_id(1)

    @pl.when(j < first_shifted)
    def _():
        o_ref[...] = jnp.dot(x_ref[...], a_ref[...], preferred_element_type=F32).astype(o_ref.dtype)

    @pl.when(j >= first_shifted)
    def _():
        w = jnp.concatenate([a_ref[:, IN_PACK_SHIFT:], b_ref[:, :IN_PACK_SHIFT]], axis=1)
        o_ref[...] = jnp.dot(x_ref[...], w, preferred_element_type=F32).astype(o_ref.dtype)


def _in_proj(x, w_in_bf, layer, tm, tn):
    m, k = x.shape
    n_packed = w_in_bf.shape[2] - IN_PACK_SHIFT
    ba_lo = B_OFF + 4 * GDN_WIDTH
    lanes = HEAD_DIM
    return pl.pallas_call(
        functools.partial(_inproj_kernel, first_shifted=ba_lo // tn),
        grid=(m // tm, n_packed // tn),
        in_specs=[pl.BlockSpec((tm, k), lambda i, j: (i, 0)),
                  pl.BlockSpec((None, k, tn), lambda i, j: (layer, 0, j)),
                  pl.BlockSpec((None, k, lanes), lambda i, j: (layer, 0, (tn // lanes) * (j + 1)))],
        out_specs=pl.BlockSpec((tm, tn), lambda i, j: (i, j)),
        out_shape=jax.ShapeDtypeStruct((m, n_packed), BF16),
        compiler_params=_cparams("parallel", "arbitrary"),
        name="in_proj",
    )(x, w_in_bf, w_in_bf)


def _interleave_gate_up(a):
    lead = a.shape[:-1]
    f = a.shape[-1] // 2
    a = a.reshape(lead + (2, f // FFN_TILE, FFN_TILE))
    a = jnp.swapaxes(a, -3, -2)
    return a.reshape(lead + (2 * f,))


def _rope_tables(s):
    half = HEAD_DIM // 2
    inv_freq = ROPE_THETA ** (-jnp.arange(half, dtype=F32) / half)
    ang = jnp.arange(s, dtype=F32)[:, None] * inv_freq[None, :]
    cos, sin = jnp.cos(ang), jnp.sin(ang)
    return jnp.concatenate([cos, cos], axis=1), jnp.concatenate([-sin, sin], axis=1)


def _swa_kernel(sink_ref, q_ref, kc_ref, kp_ref, vc_ref, vp_ref, cos_ref, sin_ref, cosp_ref, sinp_ref, o_ref):
    n = pl.program_id(1)
    blk = SWA_BLOCK
    cos, sin = cos_ref[...], sin_ref[...]
    cosp, sinp = cosp_ref[...], sinp_ref[...]
    qi = lax.broadcasted_iota(jnp.int32, (blk, 2 * blk), 0)
    kj = lax.broadcasted_iota(jnp.int32, (blk, 2 * blk), 1)
    no_prev = jnp.where(n > 0, 0, blk)
    valid = jnp.where(kj < blk, kj - qi - no_prev, qi - kj + blk + 1) > 0
    scale = HEAD_DIM ** -0.5
    for hk in range(SWA_KV_HEADS):
        sl = slice(hk * HEAD_DIM, (hk + 1) * HEAD_DIM)
        kc = _rope(kc_ref[:, sl].astype(F32), cos, sin)
        kp = _rope(kp_ref[:, sl].astype(F32), cosp, sinp)
        k2 = jnp.concatenate([kp, kc], axis=0).astype(BF16)
        v2 = jnp.concatenate([vp_ref[:, sl], vc_ref[:, sl]], axis=0)
        qs = []
        for g in range(SWA_GROUP):
            hq = hk * SWA_GROUP + g
            qh = _rope(q_ref[:, hq * HEAD_DIM:(hq + 1) * HEAD_DIM].astype(F32), cos, sin) * scale
            qs.append(qh.astype(BF16))
        scores = _mm_nt(jnp.concatenate(qs, axis=0), k2)
        probs, inv_den = [], []
        for g in range(SWA_GROUP):
            sink = sink_ref[hk * SWA_GROUP + g]
            sg = jnp.where(valid, scores[g * blk:(g + 1) * blk], MASKED)
            m = jnp.maximum(jnp.max(sg, axis=1, keepdims=True), sink)
            p = jnp.exp(sg - m)
            den = jnp.sum(p, axis=1, keepdims=True) + jnp.exp(sink - m)
            probs.append(p.astype(BF16))
            inv_den.append(1.0 / den)
        out = jnp.dot(jnp.concatenate(probs, axis=0), v2, preferred_element_type=F32)
        for g in range(SWA_GROUP):
            hq = hk * SWA_GROUP + g
            o_ref[:, hq * HEAD_DIM:(hq + 1) * HEAD_DIM] = (out[g * blk:(g + 1) * blk] * inv_den[g]).astype(o_ref.dtype)


def _swa(proj, sinks, cos_t, sin_t, b, s):
    nb = s // SWA_BLOCK
    kcol = (A_OFF + SWA_WIDTH) // SWA_KV_WIDTH
    vcol = kcol + 1
    cur = lambda bi, n: (bi * nb + n, 0)
    prev_rows = lambda bi, n: bi * nb + jnp.maximum(n - 1, 0)
    return pl.pallas_call(
        _swa_kernel,
        grid=(b, nb),
        in_specs=[pl.BlockSpec(memory_space=pltpu.SMEM),
                  pl.BlockSpec((SWA_BLOCK, SWA_WIDTH), cur),
                  pl.BlockSpec((SWA_BLOCK, SWA_KV_WIDTH), lambda bi, n: (bi * nb + n, kcol)),
                  pl.BlockSpec((SWA_BLOCK, SWA_KV_WIDTH), lambda bi, n: (prev_rows(bi, n), kcol)),
                  pl.BlockSpec((SWA_BLOCK, SWA_KV_WIDTH), lambda bi, n: (bi * nb + n, vcol)),
                  pl.BlockSpec((SWA_BLOCK, SWA_KV_WIDTH), lambda bi, n: (prev_rows(bi, n), vcol)),
                  pl.BlockSpec((SWA_BLOCK, HEAD_DIM), lambda bi, n: (n, 0)),
                  pl.BlockSpec((SWA_BLOCK, HEAD_DIM), lambda bi, n: (n, 0)),
                  pl.BlockSpec((SWA_BLOCK, HEAD_DIM), lambda bi, n: (jnp.maximum(n - 1, 0), 0)),
                  pl.BlockSpec((SWA_BLOCK, HEAD_DIM), lambda bi, n: (jnp.maximum(n - 1, 0), 0))],
        out_specs=pl.BlockSpec((SWA_BLOCK, SWA_WIDTH), cur),
        out_shape=jax.ShapeDtypeStruct((b * s, SWA_WIDTH), BF16),
        compiler_params=_cparams("parallel", "parallel"),
        name="swa",
    )(sinks, proj, proj, proj, proj, proj, cos_t, sin_t, cos_t, sin_t)


def _ret_kernel(q_ref, k_ref, v_ref, g_ref, cos_ref, sin_ref, dm_ref, xi_ref, zeta_ref, gch_ref, o_ref, state_ref):
    @pl.when(pl.program_id(1) == 0)
    def _():
        state_ref[...] = jnp.zeros(state_ref.shape, F32)

    cos, sin = cos_ref[...], sin_ref[...]
    for h in range(RET_HEADS):
        sl = slice(h * HEAD_DIM, (h + 1) * HEAD_DIM)
        q = _rope(q_ref[:, sl].astype(F32), cos, sin)
        k = _rope(k_ref[:, sl].astype(F32), cos, sin) * (HEAD_DIM ** -0.5)
        v = v_ref[:, sl]
        scores = _mm_nt(q, k) * dm_ref[h]
        inner = _mm(scores, v)
        state = state_ref[h]
        cross = _mm(q, state) * xi_ref[h]
        kz_t = jnp.transpose(k * zeta_ref[h])
        state_ref[h] = state * gch_ref[h] + _mm(kz_t, v)
        o = inner + cross
        mu = jnp.mean(o, axis=-1, keepdims=True)
        var = jnp.mean(jnp.square(o - mu), axis=-1, keepdims=True)
        o = (o - mu) * lax.rsqrt(var + NORM_EPS)
        o_ref[:, sl] = (_silu(g_ref[:, sl].astype(F32)) * o).astype(o_ref.dtype)


def _retention(proj, cos_t, sin_t, b, s):
    c = RET_CHUNK
    nc = s // c
    h = RET_HEADS
    log_gamma = jnp.log(1.0 - 2.0 ** (-5.0 - jnp.arange(h, dtype=F32)))
    idx = jnp.arange(c, dtype=F32)
    rel = idx[:, None] - idx[None, :]
    d_mask = jnp.where(rel >= 0, jnp.exp(jnp.maximum(rel, 0.0)[None] * log_gamma[:, None, None]), 0.0)
    xi = jnp.exp((idx + 1.0)[None, :] * log_gamma[:, None])
    zeta = jnp.exp((c - 1.0 - idx)[None, :] * log_gamma[:, None])
    gamma_chunk = jnp.exp(c * log_gamma)
    xi_b = jnp.broadcast_to(xi[:, :, None], (h, c, HEAD_DIM))
    zeta_b = jnp.broadcast_to(zeta[:, :, None], (h, c, HEAD_DIM))
    gch_b = jnp.broadcast_to(gamma_chunk[:, None, None], (h, 1, HEAD_DIM))
    col0 = C_OFF // RET_WIDTH
    whole3 = lambda bi, n: (0, 0, 0)
    row = lambda bi, n: bi * nc + n
    return pl.pallas_call(
        _ret_kernel,
        grid=(b, nc),
        in_specs=[pl.BlockSpec((c, RET_WIDTH), lambda bi, n: (row(bi, n), col0)),
                  pl.BlockSpec((c, RET_WIDTH), lambda bi, n: (row(bi, n), col0 + 1)),
                  pl.BlockSpec((c, RET_WIDTH), lambda bi, n: (row(bi, n), col0 + 2)),
                  pl.BlockSpec((c, RET_WIDTH), lambda bi, n: (row(bi, n), col0 + 3)),
                  pl.BlockSpec((c, HEAD_DIM), lambda bi, n: (n, 0)),
                  pl.BlockSpec((c, HEAD_DIM), lambda bi, n: (n, 0)),
                  pl.BlockSpec((h, c, c), whole3),
                  pl.BlockSpec((h, c, HEAD_DIM), whole3),
                  pl.BlockSpec((h, c, HEAD_DIM), whole3),
                  pl.BlockSpec((h, 1, HEAD_DIM), whole3)],
        out_specs=pl.BlockSpec((c, RET_WIDTH), lambda bi, n: (row(bi, n), 0)),
        out_shape=jax.ShapeDtypeStruct((b * s, RET_WIDTH), BF16),
        scratch_shapes=[pltpu.VMEM((h, HEAD_DIM, HEAD_DIM), F32)],
        compiler_params=_cparams("parallel", "arbitrary"),
        name="retention",
    )(proj, proj, proj, proj, cos_t, sin_t, d_mask, xi_b, zeta_b, gch_b)


def _softplus(x):
    return jnp.maximum(x, 0.0) + jnp.log(1.0 + jnp.exp(-jnp.abs(x)))


def _gdn_kernel(alog_ref, dtb_ref, q_ref, k_ref, v_ref, z_ref, cwq_ref, cwk_ref, cwv_ref, ba_ref, bat_ref, nw_ref,
                o_ref, cbuf_ref, state_ref, *, heads):
    hg = pl.program_id(1)
    n = pl.program_id(2)
    rows, chunk, d = GDN_ROWS, GDN_CHUNK, HEAD_DIM
    hw = heads * d

    @pl.when(n == 0)
    def _():
        state_ref[...] = jnp.zeros(state_ref.shape, F32)
        cbuf_ref[0:8, :] = jnp.zeros((8, cbuf_ref.shape[1]), F32)

    for i, (x_ref, cw_ref) in enumerate(((q_ref, cwq_ref), (k_ref, cwk_ref), (v_ref, cwv_ref))):
        cs = slice(i * hw, (i + 1) * hw)
        cbuf_ref[8:8 + rows, cs] = x_ref[...].astype(F32)
        cw = cw_ref[...]
        y = cw[3:4, :] * cbuf_ref[8:8 + rows, cs]
        for tap in range(1, GDN_CONV):
            y = y + cw[3 - tap:4 - tap, :] * cbuf_ref[8 - tap:8 - tap + rows, cs]
        cbuf_ref[0:8, cs] = cbuf_ref[rows:rows + 8, cs]
        cbuf_ref[8:8 + rows, cs] = _silu(y)

    ri = lax.broadcasted_iota(jnp.int32, (rows, rows), 0)
    ci = lax.broadcasted_iota(jnp.int32, (rows, rows), 1)

    def same(blk):
        sh = blk.bit_length() - 1
        return jnp.right_shift(ri, sh) == jnp.right_shift(ci, sh)

    causal = jnp.logical_and(same(chunk), ci <= ri)
    causal_t = jnp.logical_and(same(chunk), ri <= ci)
    strict = jnp.logical_and(same(chunk), ci < ri)
    lane = lax.broadcasted_iota(jnp.int32, (rows, BA_WIDTH), 1)
    ba = ba_ref[...]

    hr = range(heads)
    each = lambda f, *ls: [f(*a) for a in zip(*ls)]
    q = [cbuf_ref[8:8 + rows, hh * d:(hh + 1) * d] for hh in hr]
    k = [cbuf_ref[8:8 + rows, hw + hh * d:hw + (hh + 1) * d] for hh in hr]
    v = [cbuf_ref[8:8 + rows, 2 * hw + hh * d:2 * hw + (hh + 1) * d] for hh in hr]
    q = each(lambda t: t * lax.rsqrt(jnp.sum(t * t, axis=-1, keepdims=True) + NORM_EPS) * (d ** -0.5), q)
    k = each(lambda t: t * lax.rsqrt(jnp.sum(t * t, axis=-1, keepdims=True) + NORM_EPS), k)
    kb = each(lambda t: t.astype(BF16), k)

    head = [hg * heads + hh for hh in hr]
    a_neg = [-jnp.exp(alog_ref[h]) for h in head]
    dtb = [dtb_ref[h] for h in head]
    beta_c = [jax.nn.sigmoid(jnp.sum(jnp.where(lane == h, ba, 0.0), axis=1, keepdims=True)) for h in head]
    alpha_c = [jnp.sum(jnp.where(lane == GDN_HEADS + h, ba, 0.0), axis=1, keepdims=True) for h in head]
    alpha_r = [bat_ref[pl.ds(GDN_HEADS + h, 1), :] for h in head]
    g_c = each(lambda a, x, t: a * _softplus(x + t), a_neg, alpha_c, dtb)
    g_r = each(lambda a, x, t: a * _softplus(x + t), a_neg, alpha_r, dtb)
    gc_c = each(lambda g: jnp.sum(jnp.where(causal, g, 0.0), axis=1, keepdims=True), g_r)
    gc_r = each(lambda g: jnp.sum(jnp.where(causal_t, g, 0.0), axis=0, keepdims=True), g_c)
    decay = each(lambda c, r: jnp.exp(jnp.where(causal, c - r, MASKED)), gc_c, gc_r)

    low = each(lambda bc, t, dc: jnp.where(strict, bc * _mm_nt(t, t) * dc, 0.0), beta_c, kb, decay)
    dg = each(lambda t: jnp.where(same(16), t, 0.0), low)
    d2 = each(lambda t: _mm(t, t), dg)
    d4 = each(lambda t: _mm(t, t), d2)
    e1 = each(lambda a, b2: b2 - a - _mm(a, b2), dg, d2)
    d8 = each(lambda t: _mm(t, t), d4)
    e2 = each(lambda a, b2: a + b2 + _mm(a, b2), d4, d8)
    e = each(lambda a, b2: a + b2 + _mm(a, b2), e1, e2)
    for blk in (16, 32):
        off_mask = jnp.logical_and(same(2 * blk), jnp.logical_not(same(blk)))
        off = each(lambda t: jnp.where(off_mask, t, 0.0), low)
        y = each(lambda o_, e_: o_ + _mm(o_, e_), off, e)
        e = each(lambda e_, y_: e_ - y_ - _mm(e_, y_), e, y)

    rhs = each(lambda v_, k_, bc, gc: jnp.concatenate([v_ * bc, k_ * (bc * jnp.exp(gc))], axis=1), v, k, beta_c, gc_c)
    sol = each(lambda r, e_: r + _mm(e_, r), rhs, e)
    qk = each(lambda q_, t, dc: _mm_nt(q_, t) * dc, q, kb, decay)
    qg = each(lambda q_, gc: q_ * jnp.exp(gc), q, gc_c)

    state = [state_ref[hh] for hh in hr]
    v_new = [[] for _ in hr]
    cross = [[] for _ in hr]
    for c in range(rows // chunk):
        r0 = c * chunk
        rs = slice(r0, r0 + chunk)
        both = each(lambda s_, g_, st: _mm(jnp.concatenate([s_[rs, d:], g_[rs]], axis=0), st), sol, qg, state)
        vn = each(lambda s_, bt: s_[rs, :d] - bt[:chunk], sol, both)
        g_last = [gc[r0 + chunk - 1:r0 + chunk, :] for gc in gc_c]
        kd_t = each(lambda k_, gl, gc: jnp.transpose(k_[rs] * jnp.exp(gl - gc[rs])), k, g_last, gc_c)
        state = each(lambda st, gl, kt, vn_: st * jnp.exp(gl) + _mm(kt, vn_), state, g_last, kd_t, vn)
        for hh in hr:
            v_new[hh].append(vn[hh])
            cross[hh].append(both[hh][chunk:])
    for hh in hr:
        hs = slice(hh * d, (hh + 1) * d)
        state_ref[hh] = state[hh]
        o = jnp.concatenate(cross[hh], axis=0) + _mm(qk[hh], jnp.concatenate(v_new[hh], axis=0))
        o = o * lax.rsqrt(jnp.mean(o * o, axis=-1, keepdims=True) + NORM_EPS) * nw_ref[...]
        o_ref[:, hs] = (o * _silu(z_ref[:, hs].astype(F32))).astype(o_ref.dtype)


def _gdn(proj, ba, conv_w, a_log, dt_bias, norm_w, b, s):
    rows, heads = GDN_ROWS, GDN_HEADS_PER_STEP
    nt = s // rows
    hw = heads * HEAD_DIM
    groups = GDN_HEADS // heads
    col0 = B_OFF // hw
    gcol = GDN_WIDTH // hw
    row = lambda bi, hg, n: bi * nt + n
    smem = pl.BlockSpec(memory_space=pltpu.SMEM)

    def xspec(i):
        return pl.BlockSpec((rows, hw), lambda bi, hg, n: (row(bi, hg, n), col0 + i * gcol + hg))

    def cwspec(i):
        return pl.BlockSpec((GDN_CONV, hw), lambda bi, hg, n: (0, i * gcol + hg))

    return pl.pallas_call(
        functools.partial(_gdn_kernel, heads=heads),
        grid=(b, groups, nt),
        in_specs=[smem, smem, xspec(0), xspec(1), xspec(2), xspec(3), cwspec(0), cwspec(1), cwspec(2),
                  pl.BlockSpec((rows, BA_WIDTH), lambda bi, hg, n: (row(bi, hg, n), 0)),
                  pl.BlockSpec((2 * GDN_HEADS, rows), lambda bi, hg, n: (0, row(bi, hg, n))),
                  pl.BlockSpec((1, HEAD_DIM), lambda bi, hg, n: (0, 0))],
        out_specs=pl.BlockSpec((rows, hw), lambda bi, hg, n: (row(bi, hg, n), hg)),
        out_shape=jax.ShapeDtypeStruct((b * s, GDN_WIDTH), BF16),
        scratch_shapes=[pltpu.VMEM((rows + 8, 3 * hw), F32),
                        pltpu.VMEM((heads, HEAD_DIM, HEAD_DIM), F32)],
        compiler_params=_cparams("parallel", "parallel", "arbitrary"),
        name="gdn",
    )(a_log, dt_bias, proj, proj, proj, proj, conv_w, conv_w, conv_w, ba,
      jnp.transpose(ba[:, :2 * GDN_HEADS]), norm_w.reshape(1, HEAD_DIM))


def kernel(x, ln_mix_w, ln_ffn_w, w_in, gdn_conv_w, gdn_a_log, gdn_dt_bias, gdn_norm_w, attn_sinks,
           w_branch, w_out, w_up, ffn_conv_w, ffn_conv_b, w_down, ln_final_w):
    b, s, d = x.shape
    t = b * s
    depth = w_in.shape[0]
    tm = min(ROW_TILE, s)
    xf = x.reshape(t, d)
    cos_t, sin_t = _rope_tables(s)
    w_in_bf = w_in.astype(BF16)
    for layer in range(depth):
        conv_w_p = _interleave_gate_up(ffn_conv_w[layer])
        conv_b_p = _interleave_gate_up(ffn_conv_b[layer]).reshape(1, -1)

        h = _rmsnorm(xf, ln_mix_w[layer], BF16)
        n_packed = w_in.shape[2] - IN_PACK_SHIFT
        proj = _in_proj(h, w_in_bf, layer, tm, 1024 if n_packed % 1024 == 0 else 512)
        ba = _gate_proj(h, w_in_bf, layer, tm)
        out_a = _swa(proj, attn_sinks[layer], cos_t, sin_t, b, s)
        out_b = _gdn(proj, ba, gdn_conv_w[layer], gdn_a_log[layer], gdn_dt_bias[layer], gdn_norm_w[layer], b, s)
        out_c = _retention(proj, cos_t, sin_t, b, s)
        merged = _branch_merge(out_a, out_b, out_c, w_branch, layer, proj, d, tm, 512)
        xf = _matmul_residual(merged, w_out, xf, tm, 512, 1, "out_proj", layer=layer)

        h = _rmsnorm(xf, ln_ffn_w[layer], BF16)
        act = _up_glu(h, w_up, layer, conv_w_p, conv_b_p, s, tm)
        xf = _matmul_residual(act, _cast_bf16(w_down, layer, 256), xf, min(tm, 512), 512, 1, "down_proj")
    out = _rmsnorm(xf, ln_final_w, x.dtype)
    return out.reshape(b, s, d)
```
